```python
import jax, jax.numpy as jnp
from jax import lax
import numpy as np

D_MODEL = 4096
BATCH = 32
SEQ = 256
DEPTH = 4
DEC_BATCH = 2
DEC_SEQ = 2048
PAST_LEN = 512

GRID_W = 64
N_MIXERS = 3
ATTN_HEAD_DIM = 128
ATTN_HEADS = D_MODEL // ATTN_HEAD_DIM
MAX_WIN_ROWS = 8
WIN_COLS = 16
Q_BLOCK_COLS = 16
KEY_BLOCK_COLS = 2 * WIN_COLS
CTX_Q_BLOCK = 128
ATTN_SCALE = ATTN_HEAD_DIM ** -0.5
NEG_INF = -1e30
RWKV_HEAD_DIM = 64
RWKV_HEADS = D_MODEL // RWKV_HEAD_DIM
DECAY_LORA = 128
ICLR_LORA = 128
GATE_LORA = 480
LNX_EPS = 64e-5
POOL_GROUPS = 4
POOL_WINDOWS = (2, 4, 8, 16)
POOL_DIM = D_MODEL // POOL_GROUPS
D_FF = 11008
N_EXPERTS = 8
TOP_K = 2
D_FF_EXPERT = 3584
NORM_EPS = 1e-6
N_ATTN_LAYERS = (DEPTH + 2) // 3
N_RWKV_LAYERS = (DEPTH + 1) // 3
N_POOL_LAYERS = DEPTH // 3
N_DENSE_FFN = (DEPTH + 1) // 2
N_MOE_FFN = DEPTH // 2

kernel_name = 'hybrid_natten_rwkv7_pool_moe_diffusion_step'


def rms_norm(x, g):
    xf = x.astype(jnp.float32)
    y = xf * lax.rsqrt(jnp.mean(xf * xf, axis=-1, keepdims=True) + NORM_EPS)
    return (y * g.astype(jnp.float32)).astype(x.dtype)


def ada_params(cond, w, b):
    m = jax.nn.silu(cond) @ w + b
    return jnp.split(m[:, None, :], 6, axis=-1)


def modulate(x, shift, scale):
    return x * (1 + scale) + shift


def attn_context(h, w_qkv, w_o):
    b, s, _ = h.shape
    q, k, v = jnp.split(h @ w_qkv, 3, axis=-1)
    shp = (b, s, ATTN_HEADS, ATTN_HEAD_DIM)
    q, k, v = q.reshape(shp), k.reshape(shp), v.reshape(shp)
    qb = jnp.moveaxis(q.reshape(b, s // CTX_Q_BLOCK, CTX_Q_BLOCK, ATTN_HEADS, ATTN_HEAD_DIM), 1, 0)

    def block(q_blk):
        logits = jnp.einsum('bqhd,bkhd->bhqk', q_blk, k).astype(jnp.float32) * ATTN_SCALE
        p = jax.nn.softmax(logits, axis=-1).astype(v.dtype)
        return jnp.einsum('bhqk,bkhd->bqhd', p, v)

    o = jnp.moveaxis(lax.map(block, qb), 0, 1).reshape(b, s, D_MODEL)
    return o @ w_o, k, v


def attn_latent(h, k_ctx, v_ctx, w_qkv, w_o, rpb):
    b, t, _ = h.shape
    rows = t // GRID_W
    wr = min(MAX_WIN_ROWS, rows)
    ncb = GRID_W // Q_BLOCK_COLS
    q, k, v = jnp.split(h @ w_qkv, 3, axis=-1)
    grid = (b, rows, GRID_W, ATTN_HEADS, ATTN_HEAD_DIM)
    q, k, v = q.reshape(grid), k.reshape(grid), v.reshape(grid)
    j = np.arange(ncb)
    key_cols = np.clip(j * Q_BLOCK_COLS - WIN_COLS // 2, 0, GRID_W - KEY_BLOCK_COLS)[:, None] + np.arange(KEY_BLOCK_COLS)[None]
    q_cols = j[:, None] * Q_BLOCK_COLS + np.arange(Q_BLOCK_COLS)[None]
    win_start = np.clip(q_cols - WIN_COLS // 2, 0, GRID_W - WIN_COLS)
    kc = key_cols[:, None, :]
    ws = win_start[:, :, None]
    in_win = ((kc >= ws) & (kc < ws + WIN_COLS))[:, :, None, :]
    dcol = (np.clip(kc - q_cols[:, :, None], 1 - WIN_COLS, WIN_COLS - 1) + WIN_COLS - 1)[:, :, None, :]
    rpb_f = rpb.astype(jnp.float32)
    n_loc = wr * KEY_BLOCK_COLS

    def row_block(args):
        r, q_r = args
        q_r = q_r.reshape(b, ncb, Q_BLOCK_COLS, ATTN_HEADS, ATTN_HEAD_DIM)
        rs = jnp.clip(r - wr // 2, 0, rows - wr)
        kb = lax.dynamic_slice_in_dim(k, rs, wr, axis=1)[:, :, key_cols]
        vb = lax.dynamic_slice_in_dim(v, rs, wr, axis=1)[:, :, key_cols]
        drow = (rs + jnp.arange(wr) - r + MAX_WIN_ROWS - 1)[None, None, :, None]
        bias = jnp.where(in_win, rpb_f[:, drow, dcol], NEG_INF)
        s_loc = jnp.einsum('bjqhd,bajkhd->bhjqak', q_r, kb).astype(jnp.float32) * ATTN_SCALE + bias[None]
        s_loc = s_loc.reshape(b, ATTN_HEADS, ncb, Q_BLOCK_COLS, n_loc)
        s_ctx = jnp.einsum('bjqhd,bkhd->bhjqk', q_r, k_ctx).astype(jnp.float32) * ATTN_SCALE
        p = jax.nn.softmax(jnp.concatenate([s_loc, s_ctx], axis=-1), axis=-1).astype(v.dtype)
        p_loc = p[..., :n_loc].reshape(b, ATTN_HEADS, ncb, Q_BLOCK_COLS, wr, KEY_BLOCK_COLS)
        o = (jnp.einsum('bhjqak,bajkhd->bjqhd', p_loc, vb)
             + jnp.einsum('bhjqk,bkhd->bjqhd', p[..., n_loc:], v_ctx))
        return o.reshape(b, GRID_W, D_MODEL)

    o = lax.map(row_block, (jnp.arange(rows), jnp.moveaxis(q, 1, 0)))
    return jnp.moveaxis(o, 0, 1).reshape(b, t, D_MODEL) @ w_o


def token_shift_centred(x):
    prev = jnp.pad(x[:, :-1], ((0, 0), (1, 0), (0, 0)))
    nxt = jnp.pad(x[:, 1:], ((0, 0), (0, 1), (0, 0)))
    return 0.5 * (prev + nxt) - x


def rwkv_scan(s0, r, w, k, v, kk, kka, reverse):
    def step(S, inp):
        r_t, w_t, k_t, v_t, kk_t, kka_t = inp
        sa = jnp.einsum('bhij,bhj->bhi', S, kk_t)
        S = S * w_t[:, :, None, :] - sa[..., None] * kka_t[:, :, None, :] + v_t[..., None] * k_t[:, :, None, :]
        return S, jnp.einsum('bhij,bhj->bhi', S, r_t)

    xs = tuple(jnp.swapaxes(a, 0, 1) for a in (r, w, k, v, kk, kka))
    s_final, y = lax.scan(step, s0, xs, reverse=reverse)
    return jnp.swapaxes(y, 0, 1), s_final


def rwkv_mix(h, s0, mu, w_r, w_k, w_v, w_o, w0, w1, w2, a0, a1, a2, g1, g2, k_k, k_a, r_k, lnx_w, lnx_b):
    b, t, _ = h.shape
    f32 = jnp.float32

    def heads(a):
        return a.astype(f32).reshape(b, t, RWKV_HEADS, RWKV_HEAD_DIM)

    xx = token_shift_centred(h)
    xr, xw, xk, xv, xa, xg = [h + xx * mu[n] for n in range(6)]
    r = heads(xr @ w_r)
    k = heads(xk @ w_k)
    v = heads(xv @ w_v)
    g = jax.nn.sigmoid(xg @ g1) @ g2
    kk = k * k_k.astype(f32).reshape(RWKV_HEADS, RWKV_HEAD_DIM)
    kk = kk / jnp.maximum(jnp.linalg.norm(kk, axis=-1, keepdims=True), 1e-12)
    k_a_h = k_a.astype(f32).reshape(RWKV_HEADS, RWKV_HEAD_DIM)
    r_k_f = r_k.astype(f32)
    ys, bonuses, finals = [], [], []
    for d, reverse in enumerate((False, True)):
        w_log = -jax.nn.softplus(-(w0[d] + jnp.tanh(xw @ w1[d]) @ w2[d]).astype(f32)) - 0.5
        decay = jnp.exp(-jnp.exp(heads(w_log)))
        a = jax.nn.sigmoid(heads(a0[d] + (xa @ a1[d]) @ a2[d]))
        k_d = k * (1 + (a - 1) * k_a_h)
        y_d, s_d = rwkv_scan(s0[:, d].astype(f32), r, decay, k_d, v, kk, kk * a, reverse)
        ys.append(y_d)
        finals.append(s_d)
        bonuses.append(jnp.sum(r * k_d * r_k_f, axis=-1, keepdims=True) * v)
    y = ys[0] + ys[1]
    mean = jnp.mean(y, axis=-1, keepdims=True)
    var = jnp.mean(jnp.square(y - mean), axis=-1, keepdims=True)
    y = ((y - mean) * lax.rsqrt(var + LNX_EPS)).reshape(b, t, D_MODEL) * lnx_w.astype(f32) + lnx_b.astype(f32)
    y = (y + (bonuses[0] + bonuses[1]).reshape(b, t, D_MODEL)) * g.astype(f32)
    return y.astype(h.dtype) @ w_o, jnp.stack(finals, axis=1)


def pool_mix(h, w_pool, scale):
    b, t, _ = h.shape
    hf = h.astype(jnp.float32)
    csum = jnp.concatenate([jnp.zeros((b, 1, D_MODEL), jnp.float32), jnp.cumsum(hf, axis=1)], axis=1)
    pos = np.arange(t)
    outs = []
    for gi, win in enumerate(POOL_WINDOWS):
        lo = np.clip(pos - win // 2, 0, t)
        hi = np.clip(pos + win // 2, 0, t)
        cnt = jnp.asarray((hi - lo).astype(np.float32))[None, :, None]
        sl = slice(gi * POOL_DIM, (gi + 1) * POOL_DIM)
        mean = (csum[:, hi, sl] - csum[:, lo, sl]) / cnt
        outs.append((mean - hf[:, :, sl]).astype(h.dtype) @ w_pool[gi])
    return jnp.concatenate(outs, axis=-1) * scale


def swiglu(x, w_gu, w_down):
    gate, up = jnp.split(x @ w_gu, 2, axis=-1)
    return (jax.nn.silu(gate) * up) @ w_down


def moe_swiglu(x, router, w_gu, w_down):
    b, t, d = x.shape
    xf = x.reshape(b * t, d)
    probs = jax.nn.softmax((xf @ router).astype(jnp.float32), axis=-1)
    top_v, top_i = lax.top_k(probs, TOP_K)
    top_v = top_v / jnp.sum(top_v, axis=-1, keepdims=True)
    gates = jnp.sum(jax.nn.one_hot(top_i, N_EXPERTS, dtype=jnp.float32) * top_v[..., None], axis=1).astype(x.dtype)
    out = swiglu(xf, w_gu[0], w_down[0]) * gates[:, 0:1]
    for e in range(1, N_EXPERTS):
        out = out + swiglu(xf, w_gu[e], w_down[e]) * gates[:, e:e + 1]
    return out.reshape(b, t, d)


def setup_inputs(seed: int = 0) -> dict:
    key = jax.random.key(seed)
    ks = jax.random.split(key, 38)
    f32 = jnp.float32
    d = D_MODEL
    H, N = RWKV_HEADS, RWKV_HEAD_DIM

    def nrm(i, shape, scale):
        return jax.random.normal(ks[i], shape, f32) * scale

    return {
        'x_prompt': nrm(0, (BATCH, SEQ, d), 1.0),
        'x_sample': nrm(1, (DEC_BATCH, DEC_SEQ, d), 1.0),
        'cache_k': nrm(2, (DEC_BATCH, N_ATTN_LAYERS, PAST_LEN, ATTN_HEADS, ATTN_HEAD_DIM), 1.0),
        'cache_v': nrm(3, (DEC_BATCH, N_ATTN_LAYERS, PAST_LEN, ATTN_HEADS, ATTN_HEAD_DIM), 1.0),
        'state_rwkv': nrm(4, (DEC_BATCH, N_RWKV_LAYERS, 2, H, N, N), 0.5),
        'c': nrm(5, (DEC_BATCH, d), 1.0),
        'c_ctx': nrm(6, (d,), 1.0),
        'ada_w': nrm(7, (DEPTH, d, 6 * d), 0.5 * d ** -0.5),
        'ada_b': nrm(8, (DEPTH, 6 * d), 0.01),
        'norm_g': 1.0 + nrm(9, (DEPTH, 4, d), 0.05),
        'attn_w_qkv': nrm(10, (N_ATTN_LAYERS, d, 3 * d), d ** -0.5),
        'attn_w_o': nrm(11, (N_ATTN_LAYERS, d, d), d ** -0.5),
        'attn_rpb': nrm(12, (N_ATTN_LAYERS, ATTN_HEADS, 2 * MAX_WIN_ROWS - 1, 2 * WIN_COLS - 1), 0.1),
        'rwkv_mu': jax.random.uniform(ks[13], (N_RWKV_LAYERS, 6, d), f32),
        'rwkv_w_r': nrm(14, (N_RWKV_LAYERS, d, d), d ** -0.5),
        'rwkv_w_k': nrm(15, (N_RWKV_LAYERS, d, d), d ** -0.5),
        'rwkv_w_v': nrm(16, (N_RWKV_LAYERS, d, d), d ** -0.5),
        'rwkv_w_o': nrm(17, (N_RWKV_LAYERS, d, d), d ** -0.5),
        'rwkv_w0': -1.0 + nrm(18, (N_RWKV_LAYERS, 2, d), 0.5),
        'rwkv_w1': nrm(19, (N_RWKV_LAYERS, 2, d, DECAY_LORA), d ** -0.5),
        'rwkv_w2': nrm(20, (N_RWKV_LAYERS, 2, DECAY_LORA, d), 0.3 * DECAY_LORA ** -0.5),
        'rwkv_a0': nrm(21, (N_RWKV_LAYERS, 2, d), 0.1),
        'rwkv_a1': nrm(22, (N_RWKV_LAYERS, 2, d, ICLR_LORA), d ** -0.5),
        'rwkv_a2': nrm(23, (N_RWKV_LAYERS, 2, ICLR_LORA, d), 0.5 * ICLR_LORA ** -0.5),
        'rwkv_g1': nrm(24, (N_RWKV_LAYERS, d, GATE_LORA), d ** -0.5),
        'rwkv_g2': nrm(25, (N_RWKV_LAYERS, GATE_LORA, d), GATE_LORA ** -0.5),
        'rwkv_k_k': 1.0 + nrm(26, (N_RWKV_LAYERS, d), 0.05),
        'rwkv_k_a': 1.0 + nrm(27, (N_RWKV_LAYERS, d), 0.05),
        'rwkv_r_k': nrm(28, (N_RWKV_LAYERS, H, N), 0.1),
        'rwkv_lnx_w': 1.0 + nrm(29, (N_RWKV_LAYERS, d), 0.05),
        'rwkv_lnx_b': nrm(30, (N_RWKV_LAYERS, d), 0.01),
        'pool_w': nrm(31, (N_POOL_LAYERS, POOL_GROUPS, POOL_DIM, POOL_DIM), POOL_DIM ** -0.5),
        'pool_scale': 1.0 + nrm(32, (N_POOL_LAYERS, d), 0.05),
        'ffn_w_gu': nrm(33, (N_DENSE_FFN, d, 2 * D_FF), d ** -0.5),
        'ffn_w_down': nrm(34, (N_DENSE_FFN, D_FF, d), D_FF ** -0.5),
        'moe_router': nrm(35, (N_MOE_FFN, d, N_EXPERTS), d ** -0.5),
        'moe_w_gu': nrm(36, (N_MOE_FFN, N_EXPERTS, d, 2 * D_FF_EXPERT), d ** -0.5),
        'moe_w_down': nrm(37, (N_MOE_FFN, N_EXPERTS, D_FF_EXPERT, d), D_FF_EXPERT ** -0.5),
    }


def reference(x_prompt, x_sample, cache_k, cache_v, state_rwkv, c, c_ctx, ada_w, ada_b, norm_g,
              attn_w_qkv, attn_w_o, attn_rpb, rwkv_mu, rwkv_w_r, rwkv_w_k, rwkv_w_v, rwkv_w_o,
              rwkv_w0, rwkv_w1, rwkv_w2, rwkv_a0, rwkv_a1, rwkv_a2, rwkv_g1, rwkv_g2, rwkv_k_k, rwkv_k_a,
              rwkv_r_k, rwkv_lnx_w, rwkv_lnx_b, pool_w, pool_scale, ffn_w_gu, ffn_w_down,
              moe_router, moe_w_gu, moe_w_down):
    xp, xs = x_prompt, x_sample
    new_k, new_v, new_s = [], [], []
    for i in range(DEPTH):
        kind, mi, fi = i % N_MIXERS, i // N_MIXERS, i // 2
        mod_p = ada_params(c_ctx[None, :], ada_w[i], ada_b[i])
        mod_s = ada_params(c, ada_w[i], ada_b[i])
        hp = modulate(rms_norm(xp, norm_g[i, 0]), mod_p[0], mod_p[1])
        hs = modulate(rms_norm(xs, norm_g[i, 0]), mod_s[0], mod_s[1])
        if kind == 0:
            mp, kp, vp = attn_context(hp, attn_w_qkv[mi], attn_w_o[mi])
            ms = attn_latent(hs, cache_k[:, mi], cache_v[:, mi], attn_w_qkv[mi], attn_w_o[mi], attn_rpb[mi])
            new_k.append(kp)
            new_v.append(vp)
        elif kind == 1:
            rp = (rwkv_mu[mi], rwkv_w_r[mi], rwkv_w_k[mi], rwkv_w_v[mi], rwkv_w_o[mi],
                  rwkv_w0[mi], rwkv_w1[mi], rwkv_w2[mi], rwkv_a0[mi], rwkv_a1[mi], rwkv_a2[mi],
                  rwkv_g1[mi], rwkv_g2[mi], rwkv_k_k[mi], rwkv_k_a[mi], rwkv_r_k[mi],
                  rwkv_lnx_w[mi], rwkv_lnx_b[mi])
            s_zero = jnp.zeros((xp.shape[0], 2, RWKV_HEADS, RWKV_HEAD_DIM, RWKV_HEAD_DIM), jnp.float32)
            mp, sp = rwkv_mix(hp, s_zero, *rp)
            ms, _ = rwkv_mix(hs, state_rwkv[:, mi], *rp)
            new_s.append(sp)
        else:
            mp = pool_mix(hp, pool_w[mi], pool_scale[mi])
            ms = pool_mix(hs, pool_w[mi], pool_scale[mi])
        xp = xp + mod_p[2] * rms_norm(mp, norm_g[i, 1])
        xs = xs + mod_s[2] * rms_norm(ms, norm_g[i, 1])
        hp = modulate(rms_norm(xp, norm_g[i, 2]), mod_p[3], mod_p[4])
        hs = modulate(rms_norm(xs, norm_g[i, 2]), mod_s[3], mod_s[4])
        if i % 2 == 0:
            fp = swiglu(hp, ffn_w_gu[fi], ffn_w_down[fi])
            fs = swiglu(hs, ffn_w_gu[fi], ffn_w_down[fi])
        else:
            fp = moe_swiglu(hp, moe_router[fi], moe_w_gu[fi], moe_w_down[fi])
            fs = moe_swiglu(hs, moe_router[fi], moe_w_gu[fi], moe_w_down[fi])
        xp = xp + mod_p[5] * rms_norm(fp, norm_g[i, 3])
        xs = xs + mod_s[5] * rms_norm(fs, norm_g[i, 3])
    new_cache_k = jnp.stack(new_k, axis=1)
    new_cache_v = jnp.stack(new_v, axis=1)
    new_state_rwkv = jnp.stack(new_s, axis=1)
    return (xp, xs, new_cache_k, new_cache_v, new_state_rwkv)
```

```python
import functools

import numpy as np
import jax
import jax.numpy as jnp
from jax import lax
from jax.experimental import pallas as pl
from jax.experimental.pallas import tpu as pltpu

F32 = jnp.float32
BF16 = jnp.bfloat16
HIGHEST = lax.Precision.HIGHEST

V7X_VMEM_LIMIT_BYTES = 56 * 1024 * 1024
LANES = 128

NORM_EPS = 1e-6
NEG_INF = -1e30
ATTN_HEAD_DIM = 128
GRID_W = 64
MAX_WIN_ROWS = 8
WIN_COLS = 16
RWKV_HEAD_DIM = 64
RWKV_CHUNK = 64
LNX_EPS = 64e-5
POOL_WINDOWS = (2, 4, 8, 16)
POOL_HALO = 8
TOP_K = 2
TOK_BLOCK = 256


def _cparams(*sem):
    return pltpu.CompilerParams(dimension_semantics=sem, vmem_limit_bytes=V7X_VMEM_LIMIT_BYTES)


def _pick(n, candidates):
    for c in candidates:
        if c <= n and n % c == 0:
            return c
    return n


def _dot(a, b, precision=None):
    return jnp.dot(a, b, preferred_element_type=F32, precision=precision)


def _dot_nt(a, b, precision=None):
    return lax.dot_general(a, b, (((1,), (1,)), ((), ())), preferred_element_type=F32, precision=precision)


def _rms(x, g):
    return x * lax.rsqrt(jnp.mean(x * x, axis=-1, keepdims=True) + NORM_EPS) * g


def _silu(x):
    return x * jax.nn.sigmoid(x)


class _Layout:
    def __init__(self, n_prompt_seq, prompt_len, n_latent_seq, latent_len, tb):
        assert prompt_len % tb == 0 and latent_len % tb == 0
        self.tb = tb
        self.mp = n_prompt_seq * prompt_len
        self.ms = n_latent_seq * latent_len
        self.m = self.mp + self.ms
        self.prompt_len, self.latent_len = prompt_len, latent_len
        self.n_prompt_seq, self.n_latent_seq = n_prompt_seq, n_latent_seq
        rid, first, last = [], [], []
        for i in range(self.m // tb):
            row = i * tb
            if row < self.mp:
                rid.append(0)
                first.append(int(row % prompt_len == 0))
                last.append(int((row + tb) % prompt_len == 0))
            else:
                rid.append(1 + (row - self.mp) // latent_len)
                first.append(int((row - self.mp) % latent_len == 0))
                last.append(int((row - self.mp + tb) % latent_len == 0))
        self.meta = jnp.asarray(np.array([rid, first, last], np.int32))
        self.nblk = self.m // tb


def _ada_kernel(c_ref, w_ref, b_ref, o_ref):
    s = _silu(c_ref[...]).astype(BF16)
    o_ref[...] = _dot(s, w_ref[...].astype(BF16)) + b_ref[...]


def _ada_all(cond8, ada_w, ada_b):
    depth, d, n = ada_w.shape
    tn = _pick(n, (1024, 512, 256, 128))
    return pl.pallas_call(
        _ada_kernel,
        grid=(depth, n // tn),
        in_specs=[pl.BlockSpec((8, d), lambda l, j: (0, 0)),
                  pl.BlockSpec((None, d, tn), lambda l, j: (l, 0, j)),
                  pl.BlockSpec((None, 1, tn), lambda l, j: (l, 0, j))],
        out_specs=pl.BlockSpec((None, 8, tn), lambda l, j: (l, 0, j)),
        out_shape=jax.ShapeDtypeStruct((depth, 8, n), F32),
        compiler_params=_cparams("parallel", "parallel"),
        name="ada",
    )(cond8, ada_w, ada_b.reshape(depth, 1, n))


def _modulated(x, g, mod_ref, shift_row, scale_row):
    return _rms(x, g) * (1.0 + mod_ref[scale_row:scale_row + 1, :]) + mod_ref[shift_row:shift_row + 1, :]


def _router_gates(h, rw_ref, n_experts):
    logits = _dot(h, rw_ref[...], precision=HIGHEST)
    lane = lax.broadcasted_iota(jnp.int32, logits.shape, 1)
    valid = lane < n_experts
    logits = jnp.where(valid, logits, NEG_INF)
    e = jnp.exp(logits - jnp.max(logits, axis=-1, keepdims=True))
    p = e / jnp.sum(e, axis=-1, keepdims=True)
    p = jnp.where(valid, p, -2.0)
    m1 = jnp.max(p, axis=-1, keepdims=True)
    i1 = jnp.min(jnp.where(p == m1, lane, LANES), axis=-1, keepdims=True)
    p2 = jnp.where(lane == i1, -1.0, p)
    m2 = jnp.max(p2, axis=-1, keepdims=True)
    i2 = jnp.min(jnp.where(p2 == m2, lane, LANES), axis=-1, keepdims=True)
    den = m1 + m2
    return jnp.where(lane == i1, m1 / den, 0.0) + jnp.where(lane == i2, m2 / den, 0.0)


def _norm_mod_kernel(meta_ref, x_ref, g_ref, mod_ref, o_ref, *, g_row, shift_row, scale_row):
    h = _modulated(x_ref[...], g_ref[g_row:g_row + 1, :], mod_ref, shift_row, scale_row)
    o_ref[...] = h.astype(o_ref.dtype)


def _norm_mod(lay, x, g4, mods, *, g_row, shift_row, scale_row):
    m, d = x.shape
    tb = lay.tb
    return pl.pallas_call(
        functools.partial(_norm_mod_kernel, g_row=g_row, shift_row=shift_row, scale_row=scale_row),
        grid_spec=pltpu.PrefetchScalarGridSpec(
            num_scalar_prefetch=1, grid=(lay.nblk,),
            in_specs=[pl.BlockSpec((tb, d), lambda i, mt: (i, 0)),
                      pl.BlockSpec((4, d), lambda i, mt: (0, 0)),
                      pl.BlockSpec((None, 6, d), lambda i, mt: (mt[0, i], 0, 0))],
            out_specs=pl.BlockSpec((tb, d), lambda i, mt: (i, 0))),
        out_shape=jax.ShapeDtypeStruct((m, d), BF16),
        compiler_params=_cparams("parallel"),
        name="norm_mod",
    )(lay.meta, x, g4, mods)


def _resid_norm_kernel(meta_ref, x_ref, mix_ref, g_ref, mod_ref, *rest, n_experts):
    if n_experts:
        rw_ref, xo_ref, h_ref, gate_ref = rest
    else:
        xo_ref, h_ref = rest
    x = x_ref[...] + mod_ref[2:3, :] * _rms(mix_ref[...], g_ref[1:2, :])
    xo_ref[...] = x
    h = _modulated(x, g_ref[2:3, :], mod_ref, 3, 4)
    h_ref[...] = h.astype(BF16)
    if n_experts:
        gate_ref[...] = _router_gates(h, rw_ref, n_experts)


def _resid_norm(lay, x, mix, g4, mods, router=None, n_experts=0):
    m, d = x.shape
    tb = lay.tb
    row = lambda i, mt: (i, 0)
    in_specs = [pl.BlockSpec((tb, d), row), pl.BlockSpec((tb, d), row),
                pl.BlockSpec((4, d), lambda i, mt: (0, 0)),
                pl.BlockSpec((None, 6, d), lambda i, mt: (mt[0, i], 0, 0))]
    out_specs = [pl.BlockSpec((tb, d), row), pl.BlockSpec((tb, d), row)]
    out_shape = [jax.ShapeDtypeStruct((m, d), F32), jax.ShapeDtypeStruct((m, d), BF16)]
    args = [lay.meta, x, mix, g4, mods]
    if n_experts:
        in_specs.append(pl.BlockSpec((d, LANES), lambda i, mt: (0, 0)))
        out_specs.append(pl.BlockSpec((tb, LANES), row))
        out_shape.append(jax.ShapeDtypeStruct((m, LANES), F32))
        args.append(router)
    return pl.pallas_call(
        functools.partial(_resid_norm_kernel, n_experts=n_experts),
        grid_spec=pltpu.PrefetchScalarGridSpec(
            num_scalar_prefetch=1, grid=(lay.nblk,), in_specs=in_specs, out_specs=out_specs),
        out_shape=out_shape,
        compiler_params=_cparams("parallel"),
        name="resid_norm",
    )(*args)


def _resid_kernel(meta_ref, x_ref, f_ref, g_ref, mod_ref, xo_ref):
    xo_ref[...] = x_ref[...] + mod_ref[5:6, :] * _rms(f_ref[...], g_ref[3:4, :])


def _resid(lay, x, f, g4, mods):
    m, d = x.shape
    tb = lay.tb
    row = lambda i, mt: (i, 0)
    return pl.pallas_call(
        _resid_kernel,
        grid_spec=pltpu.PrefetchScalarGridSpec(
            num_scalar_prefetch=1, grid=(lay.nblk,),
            in_specs=[pl.BlockSpec((tb, d), row), pl.BlockSpec((tb, d), row),
                      pl.BlockSpec((4, d), lambda i, mt: (0, 0)),
                      pl.BlockSpec((None, 6, d), lambda i, mt: (mt[0, i], 0, 0))],
            out_specs=pl.BlockSpec((tb, d), row)),
        out_shape=jax.ShapeDtypeStruct((m, d), F32),
        compiler_params=_cparams("parallel"),
        name="resid",
    )(lay.meta, x, f, g4, mods)


def _mm_kernel(a_ref, w_ref, *rest, act, has_scale):
    if has_scale:
        s_ref, o_ref = rest
    else:
        (o_ref,) = rest
    a = a_ref[...]
    if act == "tanh":
        a = jnp.tanh(a)
    elif act == "sigmoid":
        a = jax.nn.sigmoid(a)
    out = _dot(a.astype(BF16), w_ref[...].astype(BF16))
    if has_scale:
        out = out * s_ref[...]
    o_ref[...] = out.astype(o_ref.dtype)


def _mm(a, w, lead=(), *, out_dtype=F32, act=None, a_col=0, n_groups=1, col_scale=None, tm=None, tn=None):
    m = a.shape[0]
    k, n = w.shape[-2:]
    tm = tm or _pick(m, (1024, 512, 256, 128, 64, 32, 16, 8))
    tn = tn or _pick(n, (512, 256, 128))
    nl = len(lead)
    if n_groups > 1:
        grid = (m // tm, n_groups, n // tn)
        a_spec = pl.BlockSpec((tm, k), lambda i, g, j: (i, g))
        w_spec = pl.BlockSpec((None,) * (nl + 1) + (k, tn), lambda i, g, j: lead + (g, 0, j))
        o_spec = pl.BlockSpec((tm, tn), lambda i, g, j: (i, g * (n // tn) + j))
        s_spec = pl.BlockSpec((1, tn), lambda i, g, j: (0, g * (n // tn) + j))
        sem = ("parallel", "arbitrary", "arbitrary")
    else:
        grid = (m // tm, n // tn)
        a_spec = pl.BlockSpec((tm, k), lambda i, j: (i, a_col))
        w_spec = pl.BlockSpec((None,) * nl + (k, tn), lambda i, j: lead + (0, j))
        o_spec = pl.BlockSpec((tm, tn), lambda i, j: (i, j))
        s_spec = pl.BlockSpec((1, tn), lambda i, j: (0, j))
        sem = ("parallel", "arbitrary")
    in_specs, args = [a_spec, w_spec], [a, w]
    if col_scale is not None:
        in_specs.append(s_spec)
        args.append(col_scale)
    return pl.pallas_call(
        functools.partial(_mm_kernel, act=act, has_scale=col_scale is not None),
        grid=grid, in_specs=in_specs, out_specs=o_spec,
        out_shape=jax.ShapeDtypeStruct((m, n * n_groups), out_dtype),
        compiler_params=_cparams(*sem),
        name="mm",
    )(*args)


def _mm_acc_kernel(a_ref, w_ref, o_ref):
    @pl.when(pl.program_id(2) == 0)
    def _():
        o_ref[...] = jnp.zeros_like(o_ref)

    o_ref[...] += _dot(a_ref[...], w_ref[...].astype(BF16))


def _mm_ktiled(a, w, lead, *, tk, tm=None, tn=None):
    m, k = a.shape
    n = w.shape[-1]
    tm = tm or _pick(m, (512, 256, 128, 64, 32, 16, 8))
    tn = tn or _pick(n, (512, 256, 128))
    nl = len(lead)
    return pl.pallas_call(
        _mm_acc_kernel,
        grid=(m // tm, n // tn, k // tk),
        in_specs=[pl.BlockSpec((tm, tk), lambda i, j, kk: (i, kk)),
                  pl.BlockSpec((None,) * nl + (tk, tn), lambda i, j, kk: lead + (kk, j))],
        out_specs=pl.BlockSpec((tm, tn), lambda i, j, kk: (i, j)),
        out_shape=jax.ShapeDtypeStruct((m, n), F32),
        compiler_params=_cparams("parallel", "arbitrary", "arbitrary"),
        name="mm_ktiled",
    )(a, w)


def _swiglu_kernel(a_ref, wg_ref, wu_ref, *rest, gated):
    if gated:
        gate_ref, o_ref = rest
    else:
        (o_ref,) = rest
    a = a_ref[...]
    g = _dot(a, wg_ref[...].astype(BF16))
    u = _dot(a, wu_ref[...].astype(BF16))
    h = _silu(g) * u
    if gated:
        gates = gate_ref[...]
        lane = lax.broadcasted_iota(jnp.int32, gates.shape, 1)
        h = h * jnp.sum(jnp.where(lane == pl.program_id(1), gates, 0.0), axis=-1, keepdims=True)
    o_ref[...] = h.astype(o_ref.dtype)


def _swiglu_hidden(a, w_gu, lead, *, n_experts=1, gates=None, tm=None, tn=None):
    m, d = a.shape
    f = w_gu.shape[-1] // 2
    tm = tm or _pick(m, (1024, 512, 256, 128, 64, 32, 16, 8))
    tn = tn or _pick(f, (256, 128))
    nf = f // tn
    nl = len(lead)
    if gates is not None:
        wlead = (None,) * (nl + 1)
        wg_map = lambda i, e, j: lead + (e, 0, j)
        wu_map = lambda i, e, j: lead + (e, 0, nf + j)
    else:
        wlead = (None,) * nl
        wg_map = lambda i, e, j: lead + (0, j)
        wu_map = lambda i, e, j: lead + (0, nf + j)
    in_specs = [pl.BlockSpec((tm, d), lambda i, e, j: (i, 0)),
                pl.BlockSpec(wlead + (d, tn), wg_map),
                pl.BlockSpec(wlead + (d, tn), wu_map)]
    args = [a, w_gu, w_gu]
    if gates is not None:
        in_specs.append(pl.BlockSpec((tm, LANES), lambda i, e, j: (i, 0)))
        args.append(gates)
    return pl.pallas_call(
        functools.partial(_swiglu_kernel, gated=gates is not None),
        grid=(m // tm, n_experts, nf),
        in_specs=in_specs,
        out_specs=pl.BlockSpec((tm, tn), lambda i, e, j: (i, e * nf + j)),
        out_shape=jax.ShapeDtypeStruct((m, n_experts * f), BF16),
        compiler_params=_cparams("parallel", "arbitrary", "arbitrary"),
        name="swiglu",
    )(*args)


def _softmax_rows(s):
    e = jnp.exp(s - jnp.max(s, axis=-1, keepdims=True))
    return e * (1.0 / jnp.sum(e, axis=-1, keepdims=True))


def _attn_ctx_kernel(q_ref, k_ref, v_ref, o_ref, *, heads, scale):
    for h in range(heads):
        sl = slice(h * ATTN_HEAD_DIM, (h + 1) * ATTN_HEAD_DIM)
        s = _dot_nt(q_ref[:, sl].astype(BF16), k_ref[:, sl].astype(BF16)) * scale
        p = _softmax_rows(s).astype(BF16)
        o_ref[:, sl] = _dot(p, v_ref[:, sl].astype(BF16)).astype(o_ref.dtype)


def _attn_context(qkv, n_seq, seq_len, d):
    n_heads = d // ATTN_HEAD_DIM
    hb = _pick(n_heads, (4, 2, 1))
    wb = hb * ATTN_HEAD_DIM
    ncb = d // wb
    return pl.pallas_call(
        functools.partial(_attn_ctx_kernel, heads=hb, scale=ATTN_HEAD_DIM ** -0.5),
        grid=(n_seq, ncb),
        in_specs=[pl.BlockSpec((seq_len, wb), lambda b, h: (b, h)),
                  pl.BlockSpec((seq_len, wb), lambda b, h: (b, ncb + h)),
                  pl.BlockSpec((seq_len, wb), lambda b, h: (b, 2 * ncb + h))],
        out_specs=pl.BlockSpec((seq_len, wb), lambda b, h: (b, h)),
        out_shape=jax.ShapeDtypeStruct((n_seq * seq_len, d), BF16),
        compiler_params=_cparams("parallel", "parallel"),
        name="attn_context",
    )(qkv, qkv, qkv)


def _attn_lat_kernel(q_ref, k_ref, v_ref, ck_ref, cv_ref, tb_ref, o_ref, *, rows, wr, scale):
    kb = k_ref[...].astype(BF16)
    vb = v_ref[...].astype(BF16)
    ckb = ck_ref[...].astype(BF16)
    cvb = cv_ref[...].astype(BF16)
    biases = {}
    for r in range(rows):
        rs = min(max(r - wr // 2, 0), rows - wr)
        d0 = rs - r + MAX_WIN_ROWS - 1
        if d0 not in biases:
            biases[d0] = jnp.concatenate([tb_ref[d0 + a] for a in range(wr)], axis=1)
        q = q_ref[r * GRID_W:(r + 1) * GRID_W, :].astype(BF16)
        kl = kb[rs * GRID_W:(rs + wr) * GRID_W]
        vl = vb[rs * GRID_W:(rs + wr) * GRID_W]
        s_loc = _dot_nt(q, kl) * scale + biases[d0]
        s_ctx = _dot_nt(q, ckb) * scale
        mx = jnp.maximum(jnp.max(s_loc, axis=-1, keepdims=True), jnp.max(s_ctx, axis=-1, keepdims=True))
        e_loc = jnp.exp(s_loc - mx)
        e_ctx = jnp.exp(s_ctx - mx)
        inv = 1.0 / (jnp.sum(e_loc, axis=-1, keepdims=True) + jnp.sum(e_ctx, axis=-1, keepdims=True))
        o = _dot((e_loc * inv).astype(BF16), vl) + _dot((e_ctx * inv).astype(BF16), cvb)
        o_ref[r * GRID_W:(r + 1) * GRID_W, :] = o.astype(o_ref.dtype)


def _rel_bias_table(rpb):
    qc = np.arange(GRID_W)[:, None]
    kc = np.arange(GRID_W)[None, :]
    ws = np.clip(qc - WIN_COLS // 2, 0, GRID_W - WIN_COLS)
    in_win = (kc >= ws) & (kc < ws + WIN_COLS)
    dcol = np.clip(kc - qc, 1 - WIN_COLS, WIN_COLS - 1) + WIN_COLS - 1
    return jnp.where(in_win[None, None], rpb.astype(F32)[:, :, dcol], NEG_INF)


def _attn_latent(qkv, cache_k, cache_v, mi, rpb, row_block0, n_seq, seq_len, d):
    n_heads = d // ATTN_HEAD_DIM
    rows = seq_len // GRID_W
    wr = min(MAX_WIN_ROWS, rows)
    past = cache_k.shape[2]
    ck = cache_k.reshape(cache_k.shape[0], cache_k.shape[1], past, d)
    cv = cache_v.reshape(cache_v.shape[0], cache_v.shape[1], past, d)
    table = _rel_bias_table(rpb)
    hd = ATTN_HEAD_DIM
    return pl.pallas_call(
        functools.partial(_attn_lat_kernel, rows=rows, wr=wr, scale=ATTN_HEAD_DIM ** -0.5),
        grid=(n_seq, n_heads),
        in_specs=[pl.BlockSpec((seq_len, hd), lambda b, h: (row_block0 + b, h)),
                  pl.BlockSpec((seq_len, hd), lambda b, h: (row_block0 + b, n_heads + h)),
                  pl.BlockSpec((seq_len, hd), lambda b, h: (row_block0 + b, 2 * n_heads + h)),
                  pl.BlockSpec((None, None, past, hd), lambda b, h: (b, mi, 0, h)),
                  pl.BlockSpec((None, None, past, hd), lambda b, h: (b, mi, 0, h)),
                  pl.BlockSpec((None, 2 * MAX_WIN_ROWS - 1, GRID_W, GRID_W), lambda b, h: (h, 0, 0, 0))],
        out_specs=pl.BlockSpec((seq_len, hd), lambda b, h: (b, h)),
        out_shape=jax.ShapeDtypeStruct((n_seq * seq_len, d), BF16),
        compiler_params=_cparams("parallel", "parallel"),
        name="attn_latent",
    )(qkv, qkv, qkv, ck, cv, table)


def _halo_rows(meta_ref, h_prev_blk, h_next_blk):
    i = pl.program_id(0)
    hp = jnp.where(meta_ref[1, i] == 1, 0.0, h_prev_blk[POOL_HALO - 1:POOL_HALO, :])
    hn = jnp.where(meta_ref[2, i] == 1, 0.0, h_next_blk[0:1, :])
    return hp, hn


def _rwkv_prep_kernel(meta_ref, x_ref, xp_ref, xn_ref, g_ref, mod_ref, mu_ref, *o_refs):
    g = g_ref[0:1, :]
    h = _modulated(x_ref[...], g, mod_ref, 0, 1)
    hp, hn = _halo_rows(meta_ref, _modulated(xp_ref[...], g, mod_ref, 0, 1),
                        _modulated(xn_ref[...], g, mod_ref, 0, 1))
    tb = h.shape[0]
    row = lax.broadcasted_iota(jnp.int32, h.shape, 0)
    prev = jnp.where(row == 0, hp, pltpu.roll(h, 1, 0))
    nxt = jnp.where(row == tb - 1, hn, pltpu.roll(h, tb - 1, 0))
    xx = 0.5 * (prev + nxt) - h
    for n, o_ref in enumerate(o_refs):
        o_ref[...] = (h + xx * mu_ref[n:n + 1, :]).astype(o_ref.dtype)


def _halo_specs(lay, d):
    tb = lay.tb
    per = tb // POOL_HALO
    last = lay.nblk * per - 1
    return [pl.BlockSpec((tb, d), lambda i, mt: (i, 0)),
            pl.BlockSpec((POOL_HALO, d), lambda i, mt: (jnp.maximum(i * per - 1, 0), 0)),
            pl.BlockSpec((POOL_HALO, d), lambda i, mt: (jnp.minimum((i + 1) * per, last), 0))]


def _rwkv_prep(lay, x, g4, mods, mu):
    m, d = x.shape
    tb = lay.tb
    return pl.pallas_call(
        _rwkv_prep_kernel,
        grid_spec=pltpu.PrefetchScalarGridSpec(
            num_scalar_prefetch=1, grid=(lay.nblk,),
            in_specs=_halo_specs(lay, d) + [
                pl.BlockSpec((4, d), lambda i, mt: (0, 0)),
                pl.BlockSpec((None, 6, d), lambda i, mt: (mt[0, i], 0, 0)),
                pl.BlockSpec((6, d), lambda i, mt: (0, 0))],
            out_specs=[pl.BlockSpec((tb, d), lambda i, mt: (i, 0))] * 6),
        out_shape=[jax.ShapeDtypeStruct((m, d), BF16)] * 6,
        compiler_params=_cparams("parallel"),
        name="rwkv_prep",
    )(lay.meta, x, x, x, g4, mods, mu)


def _rwkv_scan_kernel(*refs, seq_len, has_s0, prec):
    if has_s0:
        (r_ref, k_ref, v_ref, w0_ref, w1_ref, a0_ref, a1_ref, g_ref, pv_ref, s0_ref, y_ref, yf_ref, yb_ref) = refs
        sf_ref = None
    else:
        (r_ref, k_ref, v_ref, w0_ref, w1_ref, a0_ref, a1_ref, g_ref, pv_ref, y_ref, sf_ref, yf_ref, yb_ref) = refs
        s0_ref = None
    c = RWKV_CHUNK
    n = RWKV_HEAD_DIM
    nchunks = seq_len // c
    wpre_refs = (w0_ref, w1_ref)
    apre_refs = (a0_ref, a1_ref)
    k_k, k_a, r_k = pv_ref[0:1, :], pv_ref[1:2, :], pv_ref[2:3, :]
    lnx_w, lnx_b = pv_ref[3:4, :], pv_ref[4:5, :]
    w0 = (pv_ref[5:6, :], pv_ref[6:7, :])
    a0 = (pv_ref[7:8, :], pv_ref[8:9, :])

    def iota(shape, dim):
        return lax.broadcasted_iota(jnp.int32, shape, dim)

    head_of_lane = iota((c, LANES), 1) // n
    m0 = (head_of_lane == 0).astype(F32)
    m1 = 1.0 - m0
    seg = (iota((LANES, LANES), 0) // n == iota((LANES, LANES), 1) // n).astype(F32)
    tt = iota((c, 2 * c), 0)
    ss = iota((c, 2 * c), 1) % c
    lo_half = (iota((c, 2 * c), 1) < c).astype(F32)
    hi_half = 1.0 - lo_half
    strict = ((ss < tt).astype(F32), (ss > tt).astype(F32))
    incl = ((ss <= tt).astype(F32), (ss >= tt).astype(F32))
    t2 = iota((c, c), 0)
    s2 = iota((c, c), 1)
    tri = ((s2 <= t2).astype(F32), (s2 >= t2).astype(F32))

    def seg_sum(x):
        return _dot(x, seg, precision=HIGHEST)

    def stack2(x):
        return jnp.concatenate([x * m0, x * m1], axis=0)

    def features(rows, d):
        k = k_ref[rows, :]
        kk = k * k_k
        kk = kk / jnp.maximum(jnp.sqrt(seg_sum(kk * kk)), 1e-12)
        wx = -(w0[d] + wpre_refs[d][rows, :])
        w_log = -(jnp.maximum(wx, 0.0) + jnp.log(1.0 + jnp.exp(-jnp.abs(wx)))) - 0.5
        logw = -jnp.exp(w_log)
        a = jax.nn.sigmoid(a0[d] + apre_refs[d][rows, :])
        kd = k * (1.0 + (a - 1.0) * k_a)
        return kk, kd, kk * a, logw

    def chunk(rows, d, s_bd):
        r = r_ref[rows, :]
        v = v_ref[rows, :]
        kk, kd, bb, logw = features(rows, d)
        cum = _dot(tri[d], logw, precision=HIGHEST)
        total = cum[c - 1:c, :] if d == 0 else cum[0:1, :]
        e_inv = jnp.exp(-cum)
        e_rest = jnp.exp(total - cum)
        ar = jnp.concatenate([kk * jnp.exp(cum - logw), r * jnp.exp(cum)], axis=0)
        kt = kd * e_inv
        bt = bb * e_inv
        kb = jnp.concatenate([stack2(kt), stack2(bt)], axis=0)
        sc = _dot_nt(ar, kb, precision=prec)
        m_cat = sc[:c, :2 * c] * strict[d]
        l_cat = sc[:c, 2 * c:] * strict[d]
        rk_cat = sc[c:, :2 * c] * incl[d]
        rb_cat = sc[c:, 2 * c:] * incl[d]
        ars = _dot_nt(ar, s_bd, precision=prec)
        v_st = stack2(v)
        x = ars[:c] + _dot(m_cat, v_st, precision=prec)
        p = -l_cat
        n_stage = c.bit_length() - 1
        for stage in range(n_stage):
            x = x + _dot(p, stack2(x), precision=prec)
            if stage + 1 < n_stage:
                p = _dot(p, jnp.concatenate([p * lo_half, p * hi_half], axis=0), precision=prec)
        u = x
        y = ars[c:] + _dot(jnp.concatenate([rk_cat, rb_cat], axis=1),
                           jnp.concatenate([v_st, -stack2(u)], axis=0), precision=prec)
        vu_t = jnp.concatenate([v, -u], axis=0).T
        s_new = s_bd * jnp.exp(total) + seg * _dot(
            vu_t, jnp.concatenate([kd * e_rest, bb * e_rest], axis=0), precision=prec)
        return y, s_new

    if has_s0:
        z = jnp.zeros((n, n), F32)
        s_init = tuple(
            jnp.concatenate([jnp.concatenate([s0_ref[d, 0], z], axis=1),
                             jnp.concatenate([z, s0_ref[d, 1]], axis=1)], axis=0) for d in range(2))
    else:
        s_init = (jnp.zeros((LANES, LANES), F32),) * 2

    def body(ci, carry):
        s_f, s_b = carry
        rows_f = pl.ds(pl.multiple_of(ci * c, c), c)
        rows_b = pl.ds(pl.multiple_of((nchunks - 1 - ci) * c, c), c)
        y_f, s_f = chunk(rows_f, 0, s_f)
        y_b, s_b = chunk(rows_b, 1, s_b)
        yf_ref[rows_f, :] = y_f
        yb_ref[rows_b, :] = y_b
        return s_f, s_b

    s_f, s_b = lax.fori_loop(0, nchunks, body, s_init)
    if sf_ref is not None:
        for d, s_bd in enumerate((s_f, s_b)):
            sf_ref[d, 0] = s_bd[:n, :n]
            sf_ref[d, 1] = s_bd[n:, n:]

    def finish(ci, _):
        rows = pl.ds(pl.multiple_of(ci * c, c), c)
        r = r_ref[rows, :]
        v = v_ref[rows, :]
        y = yf_ref[rows, :] + yb_ref[rows, :]
        mean = seg_sum(y) * (1.0 / n)
        yc = y - mean
        var = seg_sum(yc * yc) * (1.0 / n)
        y = yc * lax.rsqrt(var + LNX_EPS) * lnx_w + lnx_b
        for d in range(2):
            _, kd, _, _ = features(rows, d)
            y = y + seg_sum(r * kd * r_k) * v
        y_ref[rows, :] = (y * g_ref[rows, :]).astype(y_ref.dtype)
        return 0

    lax.fori_loop(0, nchunks, finish, 0)


def _rwkv_scan(r, k, v, wpre, apre, g, pvec, *, row_block0, n_seq, seq_len, s0=None, mi=0, prec=HIGHEST):
    d = r.shape[1]
    npairs = d // LANES
    n = RWKV_HEAD_DIM
    seq = lambda b, p: (row_block0 + b, p)
    blk = pl.BlockSpec((seq_len, LANES), seq)
    in_specs = [blk, blk, blk,
                pl.BlockSpec((seq_len, LANES), lambda b, p: (row_block0 + b, p)),
                pl.BlockSpec((seq_len, LANES), lambda b, p: (row_block0 + b, npairs + p)),
                pl.BlockSpec((seq_len, LANES), lambda b, p: (row_block0 + b, p)),
                pl.BlockSpec((seq_len, LANES), lambda b, p: (row_block0 + b, npairs + p)),
                blk,
                pl.BlockSpec((16, LANES), lambda b, p: (0, p))]
    args = [r, k, v, wpre, wpre, apre, apre, g, pvec]
    y_shape = jax.ShapeDtypeStruct((n_seq * seq_len, d), BF16)
    y_spec = pl.BlockSpec((seq_len, LANES), lambda b, p: (b, p))
    if s0 is not None:
        in_specs.append(pl.BlockSpec((None, None, 2, 2, n, n), lambda b, p: (b, mi, 0, p, 0, 0)))
        args.append(s0)
        out_specs, out_shape = y_spec, y_shape
    else:
        out_specs = [y_spec, pl.BlockSpec((None, 2, 2, n, n), lambda b, p: (b, 0, p, 0, 0))]
        out_shape = [y_shape, jax.ShapeDtypeStruct((n_seq, 2, d // n, n, n), F32)]
    return pl.pallas_call(
        functools.partial(_rwkv_scan_kernel, seq_len=seq_len, has_s0=s0 is not None, prec=prec),
        grid=(n_seq, npairs),
        in_specs=in_specs, out_specs=out_specs, out_shape=out_shape,
        scratch_shapes=[pltpu.VMEM((seq_len, LANES), F32), pltpu.VMEM((seq_len, LANES), F32)],
        compiler_params=_cparams("parallel", "parallel"),
        name="rwkv_scan",
    )(*args)


def _pool_prep_kernel(meta_ref, x_ref, xp_ref, xn_ref, g_ref, mod_ref, o_ref):
    i = pl.program_id(0)
    g = g_ref[0:1, :]
    h = _modulated(x_ref[...], g, mod_ref, 0, 1)
    hp = jnp.where(meta_ref[1, i] == 1, 0.0, _modulated(xp_ref[...], g, mod_ref, 0, 1))
    hn = jnp.where(meta_ref[2, i] == 1, 0.0, _modulated(xn_ref[...], g, mod_ref, 0, 1))
    tb, d = h.shape
    pd = d // len(POOL_WINDOWS)
    ext = jnp.concatenate([hp, h, hn], axis=0)
    ne = tb + 2 * POOL_HALO
    row = lax.broadcasted_iota(jnp.int32, (tb, pd), 0)
    at_first = (meta_ref[1, i] == 1).astype(jnp.int32)
    at_last = (meta_ref[2, i] == 1).astype(jnp.int32)
    acc = ext[:, 0:d] + pltpu.roll(ext, 1, 0)
    half = 1
    for gi, win in enumerate(POOL_WINDOWS):
        if gi > 0:
            sub = acc[:, pd:]
            acc = pltpu.roll(sub, half, 0) + pltpu.roll(sub, ne - half, 0)
            half *= 2
        wsum = acc[POOL_HALO:POOL_HALO + tb, 0:pd]
        missing = (at_first * jnp.maximum(win // 2 - row, 0) + at_last * jnp.maximum(row + win // 2 - tb, 0))
        cnt = (win - missing).astype(F32)
        o_ref[:, gi * pd:(gi + 1) * pd] = (wsum / cnt - h[:, gi * pd:(gi + 1) * pd]).astype(o_ref.dtype)


def _pool_prep(lay, x, g4, mods):
    m, d = x.shape
    tb = lay.tb
    return pl.pallas_call(
        _pool_prep_kernel,
        grid_spec=pltpu.PrefetchScalarGridSpec(
            num_scalar_prefetch=1, grid=(lay.nblk,),
            in_specs=_halo_specs(lay, d) + [
                pl.BlockSpec((4, d), lambda i, mt: (0, 0)),
                pl.BlockSpec((None, 6, d), lambda i, mt: (mt[0, i], 0, 0))],
            out_specs=pl.BlockSpec((tb, d), lambda i, mt: (i, 0))),
        out_shape=jax.ShapeDtypeStruct((m, d), BF16),
        compiler_params=_cparams("parallel"),
        name="pool_prep",
    )(lay.meta, x, x, x, g4, mods)


def kernel(x_prompt, x_sample, cache_k, cache_v, state_rwkv, c, c_ctx, ada_w, ada_b, norm_g,
           attn_w_qkv, attn_w_o, attn_rpb, rwkv_mu, rwkv_w_r, rwkv_w_k, rwkv_w_v, rwkv_w_o,
           rwkv_w0, rwkv_w1, rwkv_w2, rwkv_a0, rwkv_a1, rwkv_a2, rwkv_g1, rwkv_g2, rwkv_k_k, rwkv_k_a,
           rwkv_r_k, rwkv_lnx_w, rwkv_lnx_b, pool_w, pool_scale, ffn_w_gu, ffn_w_down,
           moe_router, moe_w_gu, moe_w_down):
    nb, seq, d = x_prompt.shape
    ndb, dseq, _ = x_sample.shape
    depth = ada_w.shape[0]
    n_experts = moe_router.shape[-1]
    lay = _Layout(nb, seq, ndb, dseq, _pick(np.gcd(seq, dseq), (TOK_BLOCK, 128, 64, 32, 16)))
    lay_prep = _Layout(nb, seq, ndb, dseq, _pick(np.gcd(seq, dseq), (TOK_BLOCK // 2, 64, 32, 16)))
    assert lay.mp % dseq == 0
    lat_block0 = lay.mp // dseq

    x = jnp.concatenate([x_prompt.reshape(lay.mp, d), x_sample.reshape(lay.ms, d)], axis=0)
    cond8 = jnp.zeros((8, d), F32).at[0].set(c_ctx).at[1:1 + ndb].set(c)
    mods_all = _ada_all(cond8, ada_w, ada_b).reshape(depth, 8, 6, d)

    new_k, new_v, new_s = [], [], []
    for i in range(depth):
        kind, mi, fi = i % 3, i // 3, i // 2
        mods = mods_all[i]
        g4 = norm_g[i]
        if kind == 0:
            h = _norm_mod(lay, x, g4, mods, g_row=0, shift_row=0, scale_row=1)
            qkv = _mm(h, attn_w_qkv, (mi,))
            o_p = _attn_context(qkv, nb, seq, d)
            o_s = _attn_latent(qkv, cache_k, cache_v, mi, attn_rpb[mi], lat_block0, ndb, dseq, d)
            mix = _mm(jnp.concatenate([o_p, o_s], axis=0), attn_w_o, (mi,))
            heads = d // ATTN_HEAD_DIM
            new_k.append(qkv[:lay.mp, d:2 * d].reshape(nb, seq, heads, ATTN_HEAD_DIM))
            new_v.append(qkv[:lay.mp, 2 * d:].reshape(nb, seq, heads, ATTN_HEAD_DIM))
        elif kind == 1:
            xr, xw, xk, xv, xa, xg = _rwkv_prep(lay_prep, x, g4, mods, rwkv_mu[mi])
            r = _mm(xr, rwkv_w_r, (mi,))
            k = _mm(xk, rwkv_w_k, (mi,))
            v = _mm(xv, rwkv_w_v, (mi,))
            lw, la = rwkv_w1.shape[-1], rwkv_a1.shape[-1]
            w1cat = jnp.moveaxis(rwkv_w1[mi], 0, 1).reshape(d, 2 * lw)
            a1cat = jnp.moveaxis(rwkv_a1[mi], 0, 1).reshape(d, 2 * la)
            lg = rwkv_g1.shape[-1]
            lgp = -(-lg // LANES) * LANES
            g1p = jnp.pad(rwkv_g1[mi], ((0, 0), (0, lgp - lg)))
            g2p = jnp.pad(rwkv_g2[mi], ((0, lgp - lg), (0, 0)))
            tw = _mm(xw, w1cat)
            ta = _mm(xa, a1cat)
            tg = _mm(xg, g1p)
            wpre = jnp.concatenate([_mm(tw, rwkv_w2, (mi, dd), act="tanh", a_col=dd) for dd in range(2)], axis=1)
            apre = jnp.concatenate([_mm(ta, rwkv_a2, (mi, dd), a_col=dd) for dd in range(2)], axis=1)
            gate = _mm(tg, g2p, act="sigmoid")
            pvec = jnp.zeros((16, d), F32)
            for row, val in enumerate((rwkv_k_k[mi], rwkv_k_a[mi], rwkv_r_k[mi].reshape(d), rwkv_lnx_w[mi],
                                       rwkv_lnx_b[mi], rwkv_w0[mi, 0], rwkv_w0[mi, 1], rwkv_a0[mi, 0],
                                       rwkv_a0[mi, 1])):
                pvec = pvec.at[row].set(val)
            y_p, s_p = _rwkv_scan(r, k, v, wpre, apre, gate, pvec, row_block0=0, n_seq=nb, seq_len=seq)
            y_s = _rwkv_scan(r, k, v, wpre, apre, gate, pvec, row_block0=lat_block0, n_seq=ndb, seq_len=dseq,
                             s0=state_rwkv, mi=mi)
            mix = _mm(jnp.concatenate([y_p, y_s], axis=0), rwkv_w_o, (mi,))
            new_s.append(s_p)
        else:
            hd = _pool_prep(lay_prep, x, g4, mods)
            mix = _mm(hd, pool_w, (mi,), n_groups=len(POOL_WINDOWS), col_scale=pool_scale[mi].reshape(1, d))
        if i % 2 == 0:
            x, h = _resid_norm(lay, x, mix, g4, mods)
            dff = ffn_w_down.shape[1]
            hid = _swiglu_hidden(h, ffn_w_gu, (fi,))
            f = _mm_ktiled(hid, ffn_w_down, (fi,), tk=dff // 2 if (dff // 2) % LANES == 0 else dff)
        else:
            router = jnp.pad(moe_router[fi], ((0, 0), (0, LANES - n_experts)))
            x, h, gates = _resid_norm(lay, x, mix, g4, mods, router=router, n_experts=n_experts)
            fe = moe_w_down.shape[2]
            hid = _swiglu_hidden(h, moe_w_gu, (fi,), n_experts=n_experts, gates=gates)
            f = _mm_ktiled(hid, moe_w_down.reshape(moe_w_down.shape[0], n_experts * fe, d), (fi,), tk=fe)
        x = _resid(lay, x, f, g4, mods)

    y_prompt = x[:lay.mp].reshape(nb, seq, d)
    y_sample = x[lay.mp:].reshape(ndb, dseq, d)
    return (y_prompt, y_sample, jnp.stack(new_k, axis=1), jnp.stack(new_v, axis=1), jnp.stack(new_s, axis=1))
```

```python
import functools

import numpy as np
import jax
import jax.numpy as jnp
from jax import lax
from jax.experimental import pallas as pl
from jax.experimental.pallas import tpu as pltpu

F32 = jnp.float32
BF16 = jnp.bfloat16
HIGHEST = lax.Precision.HIGHEST

V7X_VMEM_LIMIT_BYTES = 56 * 1024 * 1024
RWKV_VMEM_BUDGET_BYTES = 40 * 1024 * 1024
LANES = 128

NORM_EPS = 1e-6
NEG_INF = -1e30
ATTN_HEAD_DIM = 128
GRID_W = 64
MAX_WIN_ROWS = 8
WIN_COLS = 16
RWKV_HEAD_DIM = 64
RWKV_CHUNK = 64
LNX_EPS = 64e-5
POOL_WINDOWS = (2, 4, 8, 16)
POOL_HALO = 8
TOP_K = 2
TOK_BLOCK = 256


def _cparams(*sem):
    return pltpu.CompilerParams(dimension_semantics=sem, vmem_limit_bytes=V7X_VMEM_LIMIT_BYTES)


def _pick(n, candidates):
    for c in candidates:
        if c <= n and n % c == 0:
            return c
    return n


def _dot(a, b, precision=None):
    return jnp.dot(a, b, preferred_element_type=F32, precision=precision)


def _dot_nt(a, b, precision=None):
    return lax.dot_general(a, b, (((1,), (1,)), ((), ())), preferred_element_type=F32, precision=precision)


def _rms(x, g):
    return x * lax.rsqrt(jnp.mean(x * x, axis=-1, keepdims=True) + NORM_EPS) * g


def _silu(x):
    return x * jax.nn.sigmoid(x)


class _Layout:
    def __init__(self, n_prompt_seq, prompt_len, n_latent_seq, latent_len, tb):
        assert prompt_len % tb == 0 and latent_len % tb == 0
        self.tb = tb
        self.mp = n_prompt_seq * prompt_len
        self.ms = n_latent_seq * latent_len
        self.m = self.mp + self.ms
        self.prompt_len, self.latent_len = prompt_len, latent_len
        self.n_prompt_seq, self.n_latent_seq = n_prompt_seq, n_latent_seq
        rid, first, last = [], [], []
        for i in range(self.m // tb):
            row = i * tb
            if row < self.mp:
                rid.append(0)
                first.append(int(row % prompt_len == 0))
                last.append(int((row + tb) % prompt_len == 0))
            else:
                rid.append(1 + (row - self.mp) // latent_len)
                first.append(int((row - self.mp) % latent_len == 0))
                last.append(int((row - self.mp + tb) % latent_len == 0))
        self.meta = jnp.asarray(np.array([rid, first, last], np.int32))
        self.nblk = self.m // tb


def _ada_kernel(c_ref, w_ref, b_ref, o_ref):
    s = _silu(c_ref[...]).astype(BF16)
    o_ref[...] = _dot(s, w_ref[...].astype(BF16)) + b_ref[...]


def _ada_all(cond8, ada_w, ada_b):
    depth, d, n = ada_w.shape
    tn = _pick(n, (1024, 512, 256, 128))
    return pl.pallas_call(
        _ada_kernel,
        grid=(depth, n // tn),
        in_specs=[pl.BlockSpec((8, d), lambda l, j: (0, 0)),
                  pl.BlockSpec((None, d, tn), lambda l, j: (l, 0, j)),
                  pl.BlockSpec((None, 1, tn), lambda l, j: (l, 0, j))],
        out_specs=pl.BlockSpec((None, 8, tn), lambda l, j: (l, 0, j)),
        out_shape=jax.ShapeDtypeStruct((depth, 8, n), F32),
        compiler_params=_cparams("parallel", "parallel"),
        name="ada",
    )(cond8, ada_w, ada_b.reshape(depth, 1, n))


def _modulated(x, g, mod_ref, shift_row, scale_row):
    return _rms(x, g) * (1.0 + mod_ref[scale_row:scale_row + 1, :]) + mod_ref[shift_row:shift_row + 1, :]


def _router_route(h, rw_ref, n_experts):
    logits = _dot(h, rw_ref[...], precision=HIGHEST)
    lane = lax.broadcasted_iota(jnp.int32, logits.shape, 1)
    valid = lane < n_experts
    logits = jnp.where(valid, logits, NEG_INF)
    e = jnp.exp(logits - jnp.max(logits, axis=-1, keepdims=True))
    p = e / jnp.sum(e, axis=-1, keepdims=True)
    p = jnp.where(valid, p, -2.0)
    m1 = jnp.max(p, axis=-1, keepdims=True)
    i1 = jnp.min(jnp.where(p == m1, lane, LANES), axis=-1, keepdims=True)
    p2 = jnp.where(lane == i1, -1.0, p)
    m2 = jnp.max(p2, axis=-1, keepdims=True)
    i2 = jnp.min(jnp.where(p2 == m2, lane, LANES), axis=-1, keepdims=True)
    den = m1 + m2
    return (jnp.where(lane == 0, i1.astype(F32), 0.0) + jnp.where(lane == 1, i2.astype(F32), 0.0)
            + jnp.where(lane == 2, m1 / den, 0.0) + jnp.where(lane == 3, m2 / den, 0.0))


def _norm_mod_kernel(meta_ref, x_ref, g_ref, mod_ref, o_ref, *, g_row, shift_row, scale_row):
    h = _modulated(x_ref[...], g_ref[g_row:g_row + 1, :], mod_ref, shift_row, scale_row)
    o_ref[...] = h.astype(o_ref.dtype)


def _norm_mod(lay, x, g4, mods, *, g_row, shift_row, scale_row):
    m, d = x.shape
    tb = lay.tb
    return pl.pallas_call(
        functools.partial(_norm_mod_kernel, g_row=g_row, shift_row=shift_row, scale_row=scale_row),
        grid_spec=pltpu.PrefetchScalarGridSpec(
            num_scalar_prefetch=1, grid=(lay.nblk,),
            in_specs=[pl.BlockSpec((tb, d), lambda i, mt: (i, 0)),
                      pl.BlockSpec((4, d), lambda i, mt: (0, 0)),
                      pl.BlockSpec((None, 6, d), lambda i, mt: (mt[0, i], 0, 0))],
            out_specs=pl.BlockSpec((tb, d), lambda i, mt: (i, 0))),
        out_shape=jax.ShapeDtypeStruct((m, d), BF16),
        compiler_params=_cparams("parallel"),
        name="norm_mod",
    )(lay.meta, x, g4, mods)


def _resid_norm_kernel(meta_ref, x_ref, mix_ref, g_ref, mod_ref, *rest, n_experts):
    if n_experts:
        rw_ref, xo_ref, h_ref, route_ref = rest
    else:
        xo_ref, h_ref = rest
    x = x_ref[...] + mod_ref[2:3, :] * _rms(mix_ref[...], g_ref[1:2, :])
    xo_ref[...] = x
    h = _modulated(x, g_ref[2:3, :], mod_ref, 3, 4)
    h_ref[...] = h.astype(h_ref.dtype)
    if n_experts:
        route_ref[...] = _router_route(h, rw_ref, n_experts)


def _resid_norm(lay, x, mix, g4, mods, router=None, n_experts=0):
    m, d = x.shape
    tb = lay.tb
    row = lambda i, mt: (i, 0)
    in_specs = [pl.BlockSpec((tb, d), row), pl.BlockSpec((tb, d), row),
                pl.BlockSpec((4, d), lambda i, mt: (0, 0)),
                pl.BlockSpec((None, 6, d), lambda i, mt: (mt[0, i], 0, 0))]
    out_specs = [pl.BlockSpec((tb, d), row), pl.BlockSpec((tb, d), row)]
    out_shape = [jax.ShapeDtypeStruct((m, d), F32), jax.ShapeDtypeStruct((m, d), F32 if n_experts else BF16)]
    args = [lay.meta, x, mix, g4, mods]
    if n_experts:
        in_specs.append(pl.BlockSpec((d, LANES), lambda i, mt: (0, 0)))
        out_specs.append(pl.BlockSpec((tb, LANES), row))
        out_shape.append(jax.ShapeDtypeStruct((m, LANES), F32))
        args.append(router)
    return pl.pallas_call(
        functools.partial(_resid_norm_kernel, n_experts=n_experts),
        grid_spec=pltpu.PrefetchScalarGridSpec(
            num_scalar_prefetch=1, grid=(lay.nblk,), in_specs=in_specs, out_specs=out_specs),
        out_shape=out_shape,
        compiler_params=_cparams("parallel"),
        name="resid_norm",
    )(*args)


def _resid_kernel(meta_ref, x_ref, f_ref, g_ref, mod_ref, xo_ref):
    xo_ref[...] = x_ref[...] + mod_ref[5:6, :] * _rms(f_ref[...], g_ref[3:4, :])


def _resid(lay, x, f, g4, mods):
    m, d = x.shape
    tb = lay.tb
    row = lambda i, mt: (i, 0)
    return pl.pallas_call(
        _resid_kernel,
        grid_spec=pltpu.PrefetchScalarGridSpec(
            num_scalar_prefetch=1, grid=(lay.nblk,),
            in_specs=[pl.BlockSpec((tb, d), row), pl.BlockSpec((tb, d), row),
                      pl.BlockSpec((4, d), lambda i, mt: (0, 0)),
                      pl.BlockSpec((None, 6, d), lambda i, mt: (mt[0, i], 0, 0))],
            out_specs=pl.BlockSpec((tb, d), row)),
        out_shape=jax.ShapeDtypeStruct((m, d), F32),
        compiler_params=_cparams("parallel"),
        name="resid",
    )(lay.meta, x, f, g4, mods)


def _mm_kernel(a_ref, w_ref, *rest, act, has_scale):
    if has_scale:
        s_ref, o_ref = rest
    else:
        (o_ref,) = rest
    a = a_ref[...]
    if act == "tanh":
        a = jnp.tanh(a)
    elif act == "sigmoid":
        a = jax.nn.sigmoid(a)
    out = _dot(a.astype(BF16), w_ref[...].astype(BF16))
    if has_scale:
        out = out * s_ref[...]
    o_ref[...] = out.astype(o_ref.dtype)


def _mm(a, w, lead=(), *, out_dtype=F32, act=None, a_col=0, n_groups=1, col_scale=None, tm=None, tn=None):
    m = a.shape[0]
    k, n = w.shape[-2:]
    tm = tm or _pick(m, (1024, 512, 256, 128, 64, 32, 16, 8))
    tn = tn or _pick(n, (512, 256, 128))
    nl = len(lead)
    if n_groups > 1:
        grid = (m // tm, n_groups, n // tn)
        a_spec = pl.BlockSpec((tm, k), lambda i, g, j: (i, g))
        w_spec = pl.BlockSpec((None,) * (nl + 1) + (k, tn), lambda i, g, j: lead + (g, 0, j))
        o_spec = pl.BlockSpec((tm, tn), lambda i, g, j: (i, g * (n // tn) + j))
        s_spec = pl.BlockSpec((1, tn), lambda i, g, j: (0, g * (n // tn) + j))
        sem = ("parallel", "arbitrary", "arbitrary")
    else:
        grid = (m // tm, n // tn)
        a_spec = pl.BlockSpec((tm, k), lambda i, j: (i, a_col))
        w_spec = pl.BlockSpec((None,) * nl + (k, tn), lambda i, j: lead + (0, j))
        o_spec = pl.BlockSpec((tm, tn), lambda i, j: (i, j))
        s_spec = pl.BlockSpec((1, tn), lambda i, j: (0, j))
        sem = ("parallel", "arbitrary")
    in_specs, args = [a_spec, w_spec], [a, w]
    if col_scale is not None:
        in_specs.append(s_spec)
        args.append(col_scale)
    return pl.pallas_call(
        functools.partial(_mm_kernel, act=act, has_scale=col_scale is not None),
        grid=grid, in_specs=in_specs, out_specs=o_spec,
        out_shape=jax.ShapeDtypeStruct((m, n * n_groups), out_dtype),
        compiler_params=_cparams(*sem),
        name="mm",
    )(*args)


def _swiglu_kernel(a_ref, wg_ref, wu_ref, o_ref):
    a = a_ref[...]
    g = _dot(a, wg_ref[...].astype(BF16))
    u = _dot(a, wu_ref[...].astype(BF16))
    o_ref[...] = (_silu(g) * u).astype(o_ref.dtype)


def _swiglu_hidden(a, w_gu, lead):
    m, d = a.shape
    f = w_gu.shape[-1] // 2
    tm = _pick(m, (1024, 512, 256, 128, 64, 32, 16, 8))
    tn = _pick(f, (256, 128))
    nf = f // tn
    wlead = (None,) * len(lead)
    return pl.pallas_call(
        _swiglu_kernel,
        grid=(m // tm, nf),
        in_specs=[pl.BlockSpec((tm, d), lambda i, j: (i, 0)),
                  pl.BlockSpec(wlead + (d, tn), lambda i, j: lead + (0, j)),
                  pl.BlockSpec(wlead + (d, tn), lambda i, j: lead + (0, nf + j))],
        out_specs=pl.BlockSpec((tm, tn), lambda i, j: (i, j)),
        out_shape=jax.ShapeDtypeStruct((m, f), BF16),
        compiler_params=_cparams("parallel", "arbitrary"),
        name="swiglu",
    )(a, w_gu, w_gu)


MOE_ROW_TILE = 512
MOE_GATHER_ROWS = 256


def _moe_plan(route, n_experts, tm):
    m = route.shape[0]
    ids = route[:, :TOP_K].astype(jnp.int32).reshape(-1)
    onehot = (ids[:, None] == jnp.arange(n_experts, dtype=jnp.int32)[None, :]).astype(jnp.int32)
    csum = jnp.cumsum(onehot, axis=0)
    counts = csum[-1]
    rank = jnp.sum(onehot * csum, axis=1) - 1
    tiles_per = (counts + tm - 1) // tm
    tile_end = jnp.cumsum(tiles_per)
    row_start = (tile_end - tiles_per) * tm
    pos = jnp.sum(onehot * row_start[None, :], axis=1) + rank
    n_tiles = (TOP_K * m) // tm + n_experts
    row_token = jnp.zeros((n_tiles * tm,), jnp.int32).at[pos].set(jnp.arange(TOP_K * m, dtype=jnp.int32) // TOP_K)
    t = jnp.arange(n_tiles, dtype=jnp.int32)
    n_used = tile_end[-1]
    tile_expert = jnp.sum((t[:, None] >= tile_end[None, :]).astype(jnp.int32), axis=1)
    last_expert = jnp.sum((n_used - 1 >= tile_end).astype(jnp.int32))
    tile_expert = jnp.where(t < n_used, tile_expert, last_expert)
    return row_token, tile_expert, n_used.reshape(1), pos.reshape(m, TOP_K)


def _moe_gather_kernel(nu_ref, tok_ref, h_hbm, o_ref, buf, sem, *, tiles_per_row_tile):
    i = pl.program_id(0)
    tg = buf.shape[0]

    @pl.when(i < nu_ref[0] * tiles_per_row_tile)
    def _():
        def copy(k):
            return pltpu.make_async_copy(h_hbm.at[pl.ds(tok_ref[0, k], 1), :], buf.at[pl.ds(k, 1), :], sem)

        def start(k, carry):
            copy(k).start()
            return carry

        def wait(k, carry):
            copy(k).wait()
            return carry

        lax.fori_loop(0, tg, start, 0)
        lax.fori_loop(0, tg, wait, 0)
        o_ref[...] = buf[...].astype(o_ref.dtype)

    @pl.when(i >= nu_ref[0] * tiles_per_row_tile)
    def _():
        o_ref[...] = jnp.zeros_like(o_ref)


def _moe_gather(h, row_token, n_used, tm):
    m, d = h.shape
    rows = row_token.shape[0]
    tg = _pick(tm, (MOE_GATHER_ROWS, 128, 64, 32, 16, 8))
    return pl.pallas_call(
        functools.partial(_moe_gather_kernel, tiles_per_row_tile=tm // tg),
        grid_spec=pltpu.PrefetchScalarGridSpec(
            num_scalar_prefetch=1, grid=(rows // tg,),
            in_specs=[pl.BlockSpec((None, 1, tg), lambda i, nu: (i, 0, 0), memory_space=pltpu.SMEM),
                      pl.BlockSpec(memory_space=pl.ANY)],
            out_specs=pl.BlockSpec((tg, d), lambda i, nu: (i, 0)),
            scratch_shapes=[pltpu.VMEM((tg, d), F32), pltpu.SemaphoreType.DMA(())]),
        out_shape=jax.ShapeDtypeStruct((rows, d), BF16),
        compiler_params=_cparams("arbitrary"),
        name="moe_gather",
    )(n_used, row_token.reshape(rows // tg, 1, tg), h)


def _moe_swiglu_kernel(te_ref, nu_ref, a_ref, wg_ref, wu_ref, o_ref):
    @pl.when(pl.program_id(0) < nu_ref[0])
    def _():
        a = a_ref[...]
        g = _dot(a, wg_ref[...].astype(BF16))
        u = _dot(a, wu_ref[...].astype(BF16))
        o_ref[...] = (_silu(g) * u).astype(o_ref.dtype)

    @pl.when(pl.program_id(0) >= nu_ref[0])
    def _():
        o_ref[...] = jnp.zeros_like(o_ref)


def _moe_swiglu(a, w_gu, fi, tile_expert, n_used, tm):
    rows, d = a.shape
    f = w_gu.shape[-1] // 2
    tn = _pick(f, (256, 128))
    nf = f // tn
    return pl.pallas_call(
        _moe_swiglu_kernel,
        grid_spec=pltpu.PrefetchScalarGridSpec(
            num_scalar_prefetch=2, grid=(rows // tm, nf),
            in_specs=[pl.BlockSpec((tm, d), lambda t, j, te, nu: (t, 0)),
                      pl.BlockSpec((None, None, d, tn), lambda t, j, te, nu: (fi, te[t], 0, j)),
                      pl.BlockSpec((None, None, d, tn), lambda t, j, te, nu: (fi, te[t], 0, nf + j))],
            out_specs=pl.BlockSpec((tm, tn), lambda t, j, te, nu: (t, j))),
        out_shape=jax.ShapeDtypeStruct((rows, f), BF16),
        compiler_params=_cparams("parallel", "arbitrary"),
        name="moe_swiglu",
    )(tile_expert, n_used, a, w_gu, w_gu)


def _moe_down_kernel(te_ref, nu_ref, a_ref, w_ref, o_ref):
    @pl.when(pl.program_id(0) < nu_ref[0])
    def _():
        o_ref[...] = _dot(a_ref[...], w_ref[...].astype(BF16))

    @pl.when(pl.program_id(0) >= nu_ref[0])
    def _():
        o_ref[...] = jnp.zeros_like(o_ref)


def _moe_down(a, w_down, fi, tile_expert, n_used, tm):
    rows, f = a.shape
    d = w_down.shape[-1]
    tn = _pick(d, (512, 256, 128))
    return pl.pallas_call(
        _moe_down_kernel,
        grid_spec=pltpu.PrefetchScalarGridSpec(
            num_scalar_prefetch=2, grid=(rows // tm, d // tn),
            in_specs=[pl.BlockSpec((tm, f), lambda t, j, te, nu: (t, 0)),
                      pl.BlockSpec((None, None, f, tn), lambda t, j, te, nu: (fi, te[t], 0, j))],
            out_specs=pl.BlockSpec((tm, tn), lambda t, j, te, nu: (t, j))),
        out_shape=jax.ShapeDtypeStruct((rows, d), F32),
        compiler_params=_cparams("parallel", "arbitrary"),
        name="moe_down",
    )(tile_expert, n_used, a, w_down)


def _resid_moe_kernel(meta_ref, pos_ref, x_ref, route_ref, g_ref, mod_ref, y_hbm, xo_ref, buf, sem):
    tb = x_ref.shape[0]

    def copy(k, s):
        return pltpu.make_async_copy(y_hbm.at[pl.ds(pos_ref[0, TOP_K * k + s], 1), :],
                                     buf.at[s, pl.ds(k, 1), :], sem)

    def start(k, carry):
        for s in range(TOP_K):
            copy(k, s).start()
        return carry

    def wait(k, carry):
        for s in range(TOP_K):
            copy(k, s).wait()
        return carry

    lax.fori_loop(0, tb, start, 0)
    lax.fori_loop(0, tb, wait, 0)
    route = route_ref[...]
    f = route[:, 2:3] * buf[0] + route[:, 3:4] * buf[1]
    xo_ref[...] = x_ref[...] + mod_ref[5:6, :] * _rms(f, g_ref[3:4, :])


def _resid_moe(lay, x, y_sorted, pos, route, g4, mods):
    m, d = x.shape
    tb = lay.tb
    row = lambda i, mt: (i, 0)
    return pl.pallas_call(
        _resid_moe_kernel,
        grid_spec=pltpu.PrefetchScalarGridSpec(
            num_scalar_prefetch=1, grid=(lay.nblk,),
            in_specs=[pl.BlockSpec((None, 1, TOP_K * tb), lambda i, mt: (i, 0, 0), memory_space=pltpu.SMEM),
                      pl.BlockSpec((tb, d), row), pl.BlockSpec((tb, LANES), row),
                      pl.BlockSpec((4, d), lambda i, mt: (0, 0)),
                      pl.BlockSpec((None, 6, d), lambda i, mt: (mt[0, i], 0, 0)),
                      pl.BlockSpec(memory_space=pl.ANY)],
            out_specs=pl.BlockSpec((tb, d), row),
            scratch_shapes=[pltpu.VMEM((TOP_K, tb, d), F32), pltpu.SemaphoreType.DMA(())]),
        out_shape=jax.ShapeDtypeStruct((m, d), F32),
        compiler_params=_cparams("arbitrary"),
        name="resid_moe",
    )(lay.meta, pos.reshape(lay.nblk, 1, TOP_K * tb), x, route, g4, mods, y_sorted)


def _softmax_rows(s):
    e = jnp.exp(s - jnp.max(s, axis=-1, keepdims=True))
    return e * (1.0 / jnp.sum(e, axis=-1, keepdims=True))


def _attn_ctx_kernel(q_ref, k_ref, v_ref, o_ref, *, heads, scale):
    for h in range(heads):
        sl = slice(h * ATTN_HEAD_DIM, (h + 1) * ATTN_HEAD_DIM)
        s = _dot_nt(q_ref[:, sl].astype(BF16), k_ref[:, sl].astype(BF16)) * scale
        p = _softmax_rows(s).astype(BF16)
        o_ref[:, sl] = _dot(p, v_ref[:, sl].astype(BF16)).astype(o_ref.dtype)


def _attn_context(qkv, n_seq, seq_len, d):
    n_heads = d // ATTN_HEAD_DIM
    hb = _pick(n_heads, (4, 2, 1))
    wb = hb * ATTN_HEAD_DIM
    ncb = d // wb
    return pl.pallas_call(
        functools.partial(_attn_ctx_kernel, heads=hb, scale=ATTN_HEAD_DIM ** -0.5),
        grid=(n_seq, ncb),
        in_specs=[pl.BlockSpec((seq_len, wb), lambda b, h: (b, h)),
                  pl.BlockSpec((seq_len, wb), lambda b, h: (b, ncb + h)),
                  pl.BlockSpec((seq_len, wb), lambda b, h: (b, 2 * ncb + h))],
        out_specs=pl.BlockSpec((seq_len, wb), lambda b, h: (b, h)),
        out_shape=jax.ShapeDtypeStruct((n_seq * seq_len, d), BF16),
        compiler_params=_cparams("parallel", "parallel"),
        name="attn_context",
    )(qkv, qkv, qkv)


def _attn_lat_kernel(q_ref, k_ref, v_ref, ck_ref, cv_ref, tb_ref, o_ref, *, rows, wr, scale):
    kb = k_ref[...].astype(BF16)
    vb = v_ref[...].astype(BF16)
    ckb = ck_ref[...].astype(BF16)
    cvb = cv_ref[...].astype(BF16)
    biases = {}
    for r in range(rows):
        rs = min(max(r - wr // 2, 0), rows - wr)
        d0 = rs - r + MAX_WIN_ROWS - 1
        if d0 not in biases:
            biases[d0] = jnp.concatenate([tb_ref[d0 + a] for a in range(wr)], axis=1)
        q = q_ref[r * GRID_W:(r + 1) * GRID_W, :].astype(BF16)
        kl = kb[rs * GRID_W:(rs + wr) * GRID_W]
        vl = vb[rs * GRID_W:(rs + wr) * GRID_W]
        s_loc = _dot_nt(q, kl) * scale + biases[d0]
        s_ctx = _dot_nt(q, ckb) * scale
        mx = jnp.maximum(jnp.max(s_loc, axis=-1, keepdims=True), jnp.max(s_ctx, axis=-1, keepdims=True))
        e_loc = jnp.exp(s_loc - mx)
        e_ctx = jnp.exp(s_ctx - mx)
        inv = 1.0 / (jnp.sum(e_loc, axis=-1, keepdims=True) + jnp.sum(e_ctx, axis=-1, keepdims=True))
        o = _dot((e_loc * inv).astype(BF16), vl) + _dot((e_ctx * inv).astype(BF16), cvb)
        o_ref[r * GRID_W:(r + 1) * GRID_W, :] = o.astype(o_ref.dtype)


def _rel_bias_table(rpb):
    qc = np.arange(GRID_W)[:, None]
    kc = np.arange(GRID_W)[None, :]
    ws = np.clip(qc - WIN_COLS // 2, 0, GRID_W - WIN_COLS)
    in_win = (kc >= ws) & (kc < ws + WIN_COLS)
    dcol = np.clip(kc - qc, 1 - WIN_COLS, WIN_COLS - 1) + WIN_COLS - 1
    return jnp.where(in_win[None, None], rpb.astype(F32)[:, :, dcol], NEG_INF)


def _attn_latent(qkv, cache_k, cache_v, mi, rpb, row_block0, n_seq, seq_len, d):
    n_heads = d // ATTN_HEAD_DIM
    rows = seq_len // GRID_W
    wr = min(MAX_WIN_ROWS, rows)
    past = cache_k.shape[2]
    ck = cache_k.reshape(cache_k.shape[0], cache_k.shape[1], past, d)
    cv = cache_v.reshape(cache_v.shape[0], cache_v.shape[1], past, d)
    table = _rel_bias_table(rpb)
    hd = ATTN_HEAD_DIM
    return pl.pallas_call(
        functools.partial(_attn_lat_kernel, rows=rows, wr=wr, scale=ATTN_HEAD_DIM ** -0.5),
        grid=(n_seq, n_heads),
        in_specs=[pl.BlockSpec((seq_len, hd), lambda b, h: (row_block0 + b, h)),
                  pl.BlockSpec((seq_len, hd), lambda b, h: (row_block0 + b, n_heads + h)),
                  pl.BlockSpec((seq_len, hd), lambda b, h: (row_block0 + b, 2 * n_heads + h)),
                  pl.BlockSpec((None, None, past, hd), lambda b, h: (b, mi, 0, h)),
                  pl.BlockSpec((None, None, past, hd), lambda b, h: (b, mi, 0, h)),
                  pl.BlockSpec((None, 2 * MAX_WIN_ROWS - 1, GRID_W, GRID_W), lambda b, h: (h, 0, 0, 0))],
        out_specs=pl.BlockSpec((seq_len, hd), lambda b, h: (b, h)),
        out_shape=jax.ShapeDtypeStruct((n_seq * seq_len, d), BF16),
        compiler_params=_cparams("parallel", "parallel"),
        name="attn_latent",
    )(qkv, qkv, qkv, ck, cv, table)


def _halo_rows(meta_ref, h_prev_blk, h_next_blk):
    i = pl.program_id(0)
    hp = jnp.where(meta_ref[1, i] == 1, 0.0, h_prev_blk[POOL_HALO - 1:POOL_HALO, :])
    hn = jnp.where(meta_ref[2, i] == 1, 0.0, h_next_blk[0:1, :])
    return hp, hn


def _rwkv_prep_kernel(meta_ref, x_ref, xp_ref, xn_ref, g_ref, mod_ref, mu_ref, *o_refs):
    g = g_ref[0:1, :]
    h = _modulated(x_ref[...], g, mod_ref, 0, 1)
    hp, hn = _halo_rows(meta_ref, _modulated(xp_ref[...], g, mod_ref, 0, 1),
                        _modulated(xn_ref[...], g, mod_ref, 0, 1))
    tb = h.shape[0]
    row = lax.broadcasted_iota(jnp.int32, h.shape, 0)
    prev = jnp.where(row == 0, hp, pltpu.roll(h, 1, 0))
    nxt = jnp.where(row == tb - 1, hn, pltpu.roll(h, tb - 1, 0))
    xx = 0.5 * (prev + nxt) - h
    for n, o_ref in enumerate(o_refs):
        o_ref[...] = (h + xx * mu_ref[n:n + 1, :]).astype(o_ref.dtype)


def _halo_specs(lay, d):
    tb = lay.tb
    per = tb // POOL_HALO
    last = lay.nblk * per - 1
    return [pl.BlockSpec((tb, d), lambda i, mt: (i, 0)),
            pl.BlockSpec((POOL_HALO, d), lambda i, mt: (jnp.maximum(i * per - 1, 0), 0)),
            pl.BlockSpec((POOL_HALO, d), lambda i, mt: (jnp.minimum((i + 1) * per, last), 0))]


def _rwkv_prep(lay, x, g4, mods, mu):
    m, d = x.shape
    tb = lay.tb
    return pl.pallas_call(
        _rwkv_prep_kernel,
        grid_spec=pltpu.PrefetchScalarGridSpec(
            num_scalar_prefetch=1, grid=(lay.nblk,),
            in_specs=_halo_specs(lay, d) + [
                pl.BlockSpec((4, d), lambda i, mt: (0, 0)),
                pl.BlockSpec((None, 6, d), lambda i, mt: (mt[0, i], 0, 0)),
                pl.BlockSpec((6, d), lambda i, mt: (0, 0))],
            out_specs=[pl.BlockSpec((tb, d), lambda i, mt: (i, 0))] * 6),
        out_shape=[jax.ShapeDtypeStruct((m, d), BF16)] * 6,
        compiler_params=_cparams("parallel"),
        name="rwkv_prep",
    )(lay.meta, x, x, x, g4, mods, mu)


def _split(x):
    hi = x.astype(BF16)
    return hi, (x - hi.astype(F32)).astype(BF16)


def _dot3(a, b):
    (ah, al), (bh, bl) = a, b
    return _dot(jnp.concatenate([ah, ah, al], axis=1), jnp.concatenate([bh, bl, bh], axis=0))


def _dot3_nt(a, b):
    (ah, al), (bh, bl) = a, b
    return _dot_nt(jnp.concatenate([ah, ah, al], axis=1), jnp.concatenate([bh, bl, bh], axis=1))


def _rwkv_scan_kernel(*refs, seq_len, has_s0, pairs):
    if has_s0:
        (r_ref, k_ref, v_ref, w0_ref, w1_ref, a0_ref, a1_ref, g_ref, pv_ref, s0_ref, y_ref, yf_ref, yb_ref) = refs
        sf_ref = None
    else:
        (r_ref, k_ref, v_ref, w0_ref, w1_ref, a0_ref, a1_ref, g_ref, pv_ref, y_ref, sf_ref, yf_ref, yb_ref) = refs
        s0_ref = None
    c = RWKV_CHUNK
    n = RWKV_HEAD_DIM
    nchunks = seq_len // c
    wpre_refs = (w0_ref, w1_ref)
    apre_refs = (a0_ref, a1_ref)
    def iota(shape, dim):
        return lax.broadcasted_iota(jnp.int32, shape, dim)

    head0 = iota((c, LANES), 1) < n
    seg = (iota((LANES, LANES), 0) // n == iota((LANES, LANES), 1) // n)
    seg_f = seg.astype(F32)
    seg_b = seg_f.astype(BF16)
    tt = iota((c, 2 * c), 0)
    ss = iota((c, 2 * c), 1) % c
    lo_half = iota((c, 2 * c), 1) < c
    strict = (ss < tt, ss > tt)
    incl = (ss <= tt, ss >= tt)
    t2 = iota((c, c), 0)
    s2 = iota((c, c), 1)
    tri = ((s2 <= t2).astype(F32).astype(BF16), (s2 >= t2).astype(F32).astype(BF16))

    def seg_sum(x):
        hi, lo = _split(x)
        return _dot(jnp.concatenate([hi, lo], axis=1), jnp.concatenate([seg_b, seg_b], axis=0))

    head0_b = head0.astype(F32).astype(BF16)
    head1_b = (1.0 - head0.astype(F32)).astype(BF16)
    lo_half_b = lo_half.astype(F32).astype(BF16)
    hi_half_b = (1.0 - lo_half.astype(F32)).astype(BF16)

    def stack2(x):
        return jnp.concatenate([x * head0_b, x * head1_b], axis=0)

    def stack2_parts(parts):
        return tuple(stack2(p) for p in parts)

    def blockdiag(x):
        return jnp.concatenate([x * lo_half_b, x * hi_half_b], axis=0)

    def cat0(a, b):
        return tuple(jnp.concatenate([x, y], axis=0) for x, y in zip(a, b))

    def features(rows, lanes, d):
        k = k_ref[rows, lanes]
        kk = k * pv_ref[0:1, lanes]
        kk = kk / jnp.maximum(jnp.sqrt(seg_sum(kk * kk)), 1e-12)
        wx = -(pv_ref[5 + d:6 + d, lanes] + wpre_refs[d][rows, lanes])
        w_log = -(jnp.maximum(wx, 0.0) + jnp.log(1.0 + jnp.exp(-jnp.abs(wx)))) - 0.5
        logw = -jnp.exp(w_log)
        a = jax.nn.sigmoid(pv_ref[7 + d:8 + d, lanes] + apre_refs[d][rows, lanes])
        kd = k * (1.0 + (a - 1.0) * pv_ref[1:2, lanes])
        return kk, kd, kk * a, logw

    def chunk(rows, lanes, d, s_bd):
        r = r_ref[rows, lanes]
        v = v_ref[rows, lanes]
        kk, kd, bb, logw = features(rows, lanes, d)
        yield
        l1 = logw.astype(BF16)
        l2 = (logw - l1.astype(F32)).astype(BF16)
        l3 = (logw - l1.astype(F32) - l2.astype(F32)).astype(BF16)
        cum = _dot(jnp.concatenate([tri[d]] * 3, axis=1), jnp.concatenate([l1, l2, l3], axis=0))
        yield
        total = cum[c - 1:c, :] if d == 0 else cum[0:1, :]
        e_inv = jnp.exp(-cum)
        e_rest = jnp.exp(total - cum)
        at = _split(kk * jnp.exp(cum - logw))
        rt = (r * jnp.exp(cum)).astype(BF16)
        kb = cat0(stack2_parts(_split(kd * e_inv)), stack2_parts(_split(bb * e_inv)))
        sc_a = _dot3_nt(at, kb)
        sc_r = _dot_nt(rt, kb[0])
        yield
        zero = jnp.zeros((c, 2 * c), F32)
        m_cat = jnp.where(strict[d], sc_a[:, :2 * c], zero)
        l_cat = jnp.where(strict[d], sc_a[:, 2 * c:], zero)
        rk_cat = jnp.where(incl[d], sc_r[:, :2 * c], zero)
        rb_cat = jnp.where(incl[d], sc_r[:, 2 * c:], zero)
        s_parts = _split(s_bd)
        v_st = stack2_parts(_split(v))
        x = _dot3_nt(at, s_parts) + _dot3(_split(m_cat), v_st)
        y0 = _dot_nt(rt, s_parts[0])
        yield
        p = -l_cat
        n_stage = c.bit_length() - 1
        for stage in range(n_stage):
            pp = _split(p)
            x = x + _dot3(pp, stack2_parts(_split(x)))
            if stage + 1 < n_stage:
                p = _dot3(pp, (blockdiag(pp[0]), blockdiag(pp[1])))
            yield
        u = x
        u_parts = _split(u)
        y = y0 + _dot(
            jnp.concatenate([rk_cat, rb_cat], axis=1).astype(BF16),
            jnp.concatenate([v_st[0], -stack2(u_parts[0])], axis=0))
        vu_t = _split(jnp.concatenate([v, -u], axis=0).T)
        kb_rest = cat0(_split(kd * e_rest), _split(bb * e_rest))
        s_new = s_bd * jnp.exp(total) + seg_f * _dot3(vu_t, kb_rest)
        return y, s_new

    def run_lockstep(gens):
        results = [None] * len(gens)
        active = list(range(len(gens)))
        while active:
            for i in list(active):
                try:
                    next(gens[i])
                except StopIteration as stop:
                    results[i] = stop.value
                    active.remove(i)
        return results

    chains = [(pi, d) for pi in range(pairs) for d in range(2)]
    if has_s0:
        z = jnp.zeros((n, n), F32)
        s_init = tuple(
            jnp.concatenate([jnp.concatenate([s0_ref[d, 2 * pi], z], axis=1),
                             jnp.concatenate([z, s0_ref[d, 2 * pi + 1]], axis=1)], axis=0) for pi, d in chains)
    else:
        s_init = tuple(jnp.zeros((LANES, LANES), F32) for _ in chains)

    def body(ci, carry):
        rows_fb = (pl.ds(pl.multiple_of(ci * c, c), c), pl.ds(pl.multiple_of((nchunks - 1 - ci) * c, c), c))
        lanes = [slice(pi * LANES, (pi + 1) * LANES) for pi, _ in chains]
        results = run_lockstep([chunk(rows_fb[d], ln, d, s_bd)
                                for (_, d), ln, s_bd in zip(chains, lanes, carry)])
        for (_, d), ln, (y, _) in zip(chains, lanes, results):
            (yf_ref, yb_ref)[d][rows_fb[d], ln] = y
        return tuple(s_bd for _, s_bd in results)

    s_fin = lax.fori_loop(0, nchunks, body, s_init)
    if sf_ref is not None:
        for (pi, d), s_bd in zip(chains, s_fin):
            sf_ref[d, 2 * pi] = s_bd[:n, :n]
            sf_ref[d, 2 * pi + 1] = s_bd[n:, n:]

    def finish(ci, _):
        rows = pl.ds(pl.multiple_of(ci * c, c), c)
        for pi in range(pairs):
            lanes = slice(pi * LANES, (pi + 1) * LANES)
            r = r_ref[rows, lanes]
            v = v_ref[rows, lanes]
            y = yf_ref[rows, lanes] + yb_ref[rows, lanes]
            mean = seg_sum(y) * (1.0 / n)
            yc = y - mean
            var = seg_sum(yc * yc) * (1.0 / n)
            y = yc * lax.rsqrt(var + LNX_EPS) * pv_ref[3:4, lanes] + pv_ref[4:5, lanes]
            for d in range(2):
                _, kd, _, _ = features(rows, lanes, d)
                y = y + seg_sum(r * kd * pv_ref[2:3, lanes]) * v
            y_ref[rows, lanes] = (y * g_ref[rows, lanes]).astype(y_ref.dtype)
        return 0

    lax.fori_loop(0, nchunks, finish, 0)


def _rwkv_scan(r, k, v, wpre, apre, g, pvec, *, row_block0, n_seq, seq_len, s0=None, mi=0):
    d = r.shape[1]
    n = RWKV_HEAD_DIM
    n_f32_blocks = 2 * 8 + 2 + 1
    pairs = next(p for p in (4, 2, 1)
                 if (d // LANES) % p == 0 and n_f32_blocks * seq_len * p * LANES * 4 <= RWKV_VMEM_BUDGET_BYTES)
    wb = pairs * LANES
    nblk = d // wb
    blk = pl.BlockSpec((seq_len, wb), lambda b, p: (row_block0 + b, p))
    blk_d1 = pl.BlockSpec((seq_len, wb), lambda b, p: (row_block0 + b, nblk + p))
    in_specs = [blk, blk, blk, blk, blk_d1, blk, blk_d1, blk,
                pl.BlockSpec((16, wb), lambda b, p: (0, p))]
    args = [r, k, v, wpre, wpre, apre, apre, g, pvec]
    y_shape = jax.ShapeDtypeStruct((n_seq * seq_len, d), BF16)
    y_spec = pl.BlockSpec((seq_len, wb), lambda b, p: (b, p))
    if s0 is not None:
        in_specs.append(pl.BlockSpec((None, None, 2, 2 * pairs, n, n), lambda b, p: (b, mi, 0, p, 0, 0)))
        args.append(s0)
        out_specs, out_shape = y_spec, y_shape
    else:
        out_specs = [y_spec, pl.BlockSpec((None, 2, 2 * pairs, n, n), lambda b, p: (b, 0, p, 0, 0))]
        out_shape = [y_shape, jax.ShapeDtypeStruct((n_seq, 2, d // n, n, n), F32)]
    return pl.pallas_call(
        functools.partial(_rwkv_scan_kernel, seq_len=seq_len, has_s0=s0 is not None, pairs=pairs),
        grid=(n_seq, nblk),
        in_specs=in_specs, out_specs=out_specs, out_shape=out_shape,
        scratch_shapes=[pltpu.VMEM((seq_len, wb), F32), pltpu.VMEM((seq_len, wb), F32)],
        compiler_params=_cparams("parallel", "parallel"),
        name="rwkv_scan",
    )(*args)


def _pool_prep_kernel(meta_ref, x_ref, xp_ref, xn_ref, g_ref, mod_ref, o_ref):
    i = pl.program_id(0)
    g = g_ref[0:1, :]
    h = _modulated(x_ref[...], g, mod_ref, 0, 1)
    hp = jnp.where(meta_ref[1, i] == 1, 0.0, _modulated(xp_ref[...], g, mod_ref, 0, 1))
    hn = jnp.where(meta_ref[2, i] == 1, 0.0, _modulated(xn_ref[...], g, mod_ref, 0, 1))
    tb, d = h.shape
    pd = d // len(POOL_WINDOWS)
    ext = jnp.concatenate([hp, h, hn], axis=0)
    ne = tb + 2 * POOL_HALO
    row = lax.broadcasted_iota(jnp.int32, (tb, pd), 0)
    at_first = (meta_ref[1, i] == 1).astype(jnp.int32)
    at_last = (meta_ref[2, i] == 1).astype(jnp.int32)
    acc = ext[:, 0:d] + pltpu.roll(ext, 1, 0)
    half = 1
    for gi, win in enumerate(POOL_WINDOWS):
        if gi > 0:
            sub = acc[:, pd:]
            acc = pltpu.roll(sub, half, 0) + pltpu.roll(sub, ne - half, 0)
            half *= 2
        wsum = acc[POOL_HALO:POOL_HALO + tb, 0:pd]
        missing = (at_first * jnp.maximum(win // 2 - row, 0) + at_last * jnp.maximum(row + win // 2 - tb, 0))
        cnt = (win - missing).astype(F32)
        o_ref[:, gi * pd:(gi + 1) * pd] = (wsum / cnt - h[:, gi * pd:(gi + 1) * pd]).astype(o_ref.dtype)


def _pool_prep(lay, x, g4, mods):
    m, d = x.shape
    tb = lay.tb
    return pl.pallas_call(
        _pool_prep_kernel,
        grid_spec=pltpu.PrefetchScalarGridSpec(
            num_scalar_prefetch=1, grid=(lay.nblk,),
            in_specs=_halo_specs(lay, d) + [
                pl.BlockSpec((4, d), lambda i, mt: (0, 0)),
                pl.BlockSpec((None, 6, d), lambda i, mt: (mt[0, i], 0, 0))],
            out_specs=pl.BlockSpec((tb, d), lambda i, mt: (i, 0))),
        out_shape=jax.ShapeDtypeStruct((m, d), BF16),
        compiler_params=_cparams("parallel"),
        name="pool_prep",
    )(lay.meta, x, x, x, g4, mods)


def kernel(x_prompt, x_sample, cache_k, cache_v, state_rwkv, c, c_ctx, ada_w, ada_b, norm_g,
           attn_w_qkv, attn_w_o, attn_rpb, rwkv_mu, rwkv_w_r, rwkv_w_k, rwkv_w_v, rwkv_w_o,
           rwkv_w0, rwkv_w1, rwkv_w2, rwkv_a0, rwkv_a1, rwkv_a2, rwkv_g1, rwkv_g2, rwkv_k_k, rwkv_k_a,
           rwkv_r_k, rwkv_lnx_w, rwkv_lnx_b, pool_w, pool_scale, ffn_w_gu, ffn_w_down,
           moe_router, moe_w_gu, moe_w_down):
    nb, seq, d = x_prompt.shape
    ndb, dseq, _ = x_sample.shape
    depth = ada_w.shape[0]
    n_experts = moe_router.shape[-1]
    lay = _Layout(nb, seq, ndb, dseq, _pick(np.gcd(seq, dseq), (TOK_BLOCK, 128, 64, 32, 16)))
    lay_prep = _Layout(nb, seq, ndb, dseq, _pick(np.gcd(seq, dseq), (TOK_BLOCK // 2, 64, 32, 16)))
    assert lay.mp % dseq == 0
    lat_block0 = lay.mp // dseq

    x = jnp.concatenate([x_prompt.reshape(lay.mp, d), x_sample.reshape(lay.ms, d)], axis=0)
    cond8 = jnp.zeros((8, d), F32).at[0].set(c_ctx).at[1:1 + ndb].set(c)
    mods_all = _ada_all(cond8, ada_w, ada_b).reshape(depth, 8, 6, d)

    new_k, new_v, new_s = [], [], []
    for i in range(depth):
        kind, mi, fi = i % 3, i // 3, i // 2
        mods = mods_all[i]
        g4 = norm_g[i]
        if kind == 0:
            h = _norm_mod(lay, x, g4, mods, g_row=0, shift_row=0, scale_row=1)
            qkv = _mm(h, attn_w_qkv, (mi,))
            o_p = _attn_context(qkv, nb, seq, d)
            o_s = _attn_latent(qkv, cache_k, cache_v, mi, attn_rpb[mi], lat_block0, ndb, dseq, d)
            mix = _mm(jnp.concatenate([o_p, o_s], axis=0), attn_w_o, (mi,))
            heads = d // ATTN_HEAD_DIM
            new_k.append(qkv[:lay.mp, d:2 * d].reshape(nb, seq, heads, ATTN_HEAD_DIM))
            new_v.append(qkv[:lay.mp, 2 * d:].reshape(nb, seq, heads, ATTN_HEAD_DIM))
        elif kind == 1:
            xr, xw, xk, xv, xa, xg = _rwkv_prep(lay_prep, x, g4, mods, rwkv_mu[mi])
            r = _mm(xr, rwkv_w_r, (mi,))
            k = _mm(xk, rwkv_w_k, (mi,))
            v = _mm(xv, rwkv_w_v, (mi,))
            lw, la = rwkv_w1.shape[-1], rwkv_a1.shape[-1]
            w1cat = jnp.moveaxis(rwkv_w1[mi], 0, 1).reshape(d, 2 * lw)
            a1cat = jnp.moveaxis(rwkv_a1[mi], 0, 1).reshape(d, 2 * la)
            lg = rwkv_g1.shape[-1]
            lgp = -(-lg // LANES) * LANES
            g1p = jnp.pad(rwkv_g1[mi], ((0, 0), (0, lgp - lg)))
            g2p = jnp.pad(rwkv_g2[mi], ((0, lgp - lg), (0, 0)))
            tw = _mm(xw, w1cat)
            ta = _mm(xa, a1cat)
            tg = _mm(xg, g1p)
            wpre = jnp.concatenate([_mm(tw, rwkv_w2, (mi, dd), act="tanh", a_col=dd) for dd in range(2)], axis=1)
            apre = jnp.concatenate([_mm(ta, rwkv_a2, (mi, dd), a_col=dd) for dd in range(2)], axis=1)
            gate = _mm(tg, g2p, act="sigmoid")
            pvec = jnp.zeros((16, d), F32)
            for row, val in enumerate((rwkv_k_k[mi], rwkv_k_a[mi], rwkv_r_k[mi].reshape(d), rwkv_lnx_w[mi],
                                       rwkv_lnx_b[mi], rwkv_w0[mi, 0], rwkv_w0[mi, 1], rwkv_a0[mi, 0],
                                       rwkv_a0[mi, 1])):
                pvec = pvec.at[row].set(val)
            y_p, s_p = _rwkv_scan(r, k, v, wpre, apre, gate, pvec, row_block0=0, n_seq=nb, seq_len=seq)
            y_s = _rwkv_scan(r, k, v, wpre, apre, gate, pvec, row_block0=lat_block0, n_seq=ndb, seq_len=dseq,
                             s0=state_rwkv, mi=mi)
            mix = _mm(jnp.concatenate([y_p, y_s], axis=0), rwkv_w_o, (mi,))
            new_s.append(s_p)
        else:
            hd = _pool_prep(lay_prep, x, g4, mods)
            mix = _mm(hd, pool_w, (mi,), n_groups=len(POOL_WINDOWS), col_scale=pool_scale[mi].reshape(1, d))
        if i % 2 == 0:
            x, h = _resid_norm(lay, x, mix, g4, mods)
            hid = _swiglu_hidden(h, ffn_w_gu, (fi,))
            f = _mm(hid, ffn_w_down, (fi,), tm=_pick(lay.m, (512, 256, 128, 64, 32, 16, 8)), tn=_pick(d, (256, 128)))
        else:
            router = jnp.pad(moe_router[fi], ((0, 0), (0, LANES - n_experts)))
            x, h, route = _resid_norm(lay, x, mix, g4, mods, router=router, n_experts=n_experts)
            tm = _pick(TOP_K * lay.m, (MOE_ROW_TILE, 256, 128, 64, 32, 16, 8))
            row_token, tile_expert, n_used, pos = _moe_plan(route, n_experts, tm)
            xs = _moe_gather(h, row_token, n_used, tm)
            hid = _moe_swiglu(xs, moe_w_gu, fi, tile_expert, n_used, tm)
            ys = _moe_down(hid, moe_w_down, fi, tile_expert, n_used, tm)
            x = _resid_moe(lay, x, ys, pos, route, g4, mods)
            continue
        x = _resid(lay, x, f, g4, mods)

    y_prompt = x[:lay.mp].reshape(nb, seq, d)
    y_sample = x[lay.mp:].reshape(ndb, dseq, d)
    return (y_prompt, y_sample, jnp.stack(new_k, axis=1), jnp.stack(new_v, axis=1), jnp.stack(new_s, axis=1))
```

```python
import functools

import numpy as np
import jax
import jax.numpy as jnp
from jax import lax
from jax.experimental import pallas as pl
from jax.experimental.pallas import tpu as pltpu

F32 = jnp.float32
BF16 = jnp.bfloat16
HIGHEST = lax.Precision.HIGHEST

V7X_VMEM_LIMIT_BYTES = 56 * 1024 * 1024
RWKV_VMEM_BUDGET_BYTES = 40 * 1024 * 1024
LANES = 128

NORM_EPS = 1e-6
NEG_INF = -1e30
ATTN_HEAD_DIM = 128
GRID_W = 64
MAX_WIN_ROWS = 8
WIN_COLS = 16
RWKV_HEAD_DIM = 64
RWKV_CHUNK = 64
LNX_EPS = 64e-5
POOL_WINDOWS = (2, 4, 8, 16)
POOL_HALO = 8
TOP_K = 2
TOK_BLOCK = 256


def _cparams(*sem):
    return pltpu.CompilerParams(dimension_semantics=sem, vmem_limit_bytes=V7X_VMEM_LIMIT_BYTES)


def _pick(n, candidates):
    for c in candidates:
        if c <= n and n % c == 0:
            return c
    return n


def _dot(a, b, precision=None):
    return jnp.dot(a, b, preferred_element_type=F32, precision=precision)


def _dot_nt(a, b, precision=None):
    return lax.dot_general(a, b, (((1,), (1,)), ((), ())), preferred_element_type=F32, precision=precision)


def _rms(x, g):
    return x * lax.rsqrt(jnp.mean(x * x, axis=-1, keepdims=True) + NORM_EPS) * g


def _silu(x):
    return x * jax.nn.sigmoid(x)


class _Layout:
    def __init__(self, n_prompt_seq, prompt_len, n_latent_seq, latent_len, tb):
        assert prompt_len % tb == 0 and latent_len % tb == 0
        self.tb = tb
        self.mp = n_prompt_seq * prompt_len
        self.ms = n_latent_seq * latent_len
        self.m = self.mp + self.ms
        self.prompt_len, self.latent_len = prompt_len, latent_len
        self.n_prompt_seq, self.n_latent_seq = n_prompt_seq, n_latent_seq
        rid, first, last = [], [], []
        for i in range(self.m // tb):
            row = i * tb
            if row < self.mp:
                rid.append(0)
                first.append(int(row % prompt_len == 0))
                last.append(int((row + tb) % prompt_len == 0))
            else:
                rid.append(1 + (row - self.mp) // latent_len)
                first.append(int((row - self.mp) % latent_len == 0))
                last.append(int((row - self.mp + tb) % latent_len == 0))
        self.meta = jnp.asarray(np.array([rid, first, last], np.int32))
        self.nblk = self.m // tb


def _ada_kernel(c_ref, w_ref, b_ref, o_ref):
    s = _silu(c_ref[...]).astype(BF16)
    o_ref[...] = _dot(s, w_ref[...].astype(BF16)) + b_ref[...]


def _ada_all(cond8, ada_w, ada_b):
    depth, d, n = ada_w.shape
    tn = _pick(n, (1024, 512, 256, 128))
    return pl.pallas_call(
        _ada_kernel,
        grid=(depth, n // tn),
        in_specs=[pl.BlockSpec((8, d), lambda l, j: (0, 0)),
                  pl.BlockSpec((None, d, tn), lambda l, j: (l, 0, j)),
                  pl.BlockSpec((None, 1, tn), lambda l, j: (l, 0, j))],
        out_specs=pl.BlockSpec((None, 8, tn), lambda l, j: (l, 0, j)),
        out_shape=jax.ShapeDtypeStruct((depth, 8, n), F32),
        compiler_params=_cparams("parallel", "parallel"),
        name="ada",
    )(cond8, ada_w, ada_b.reshape(depth, 1, n))


def _modulated(x, g, mod_ref, shift_row, scale_row):
    return _rms(x, g) * (1.0 + mod_ref[scale_row:scale_row + 1, :]) + mod_ref[shift_row:shift_row + 1, :]


def _router_route(h, rw_ref, n_experts):
    logits = _dot(h, rw_ref[...], precision=HIGHEST)
    lane = lax.broadcasted_iota(jnp.int32, logits.shape, 1)
    valid = lane < n_experts
    logits = jnp.where(valid, logits, NEG_INF)
    e = jnp.exp(logits - jnp.max(logits, axis=-1, keepdims=True))
    p = e / jnp.sum(e, axis=-1, keepdims=True)
    p = jnp.where(valid, p, -2.0)
    m1 = jnp.max(p, axis=-1, keepdims=True)
    i1 = jnp.min(jnp.where(p == m1, lane, LANES), axis=-1, keepdims=True)
    p2 = jnp.where(lane == i1, -1.0, p)
    m2 = jnp.max(p2, axis=-1, keepdims=True)
    i2 = jnp.min(jnp.where(p2 == m2, lane, LANES), axis=-1, keepdims=True)
    den = m1 + m2
    return (jnp.where(lane == 0, i1.astype(F32), 0.0) + jnp.where(lane == 1, i2.astype(F32), 0.0)
            + jnp.where(lane == 2, m1 / den, 0.0) + jnp.where(lane == 3, m2 / den, 0.0))


def _norm_mod_kernel(meta_ref, x_ref, g_ref, mod_ref, o_ref, *, g_row, shift_row, scale_row):
    h = _modulated(x_ref[...], g_ref[g_row:g_row + 1, :], mod_ref, shift_row, scale_row)
    o_ref[...] = h.astype(o_ref.dtype)


def _norm_mod(lay, x, g4, mods, *, g_row, shift_row, scale_row):
    m, d = x.shape
    tb = lay.tb
    return pl.pallas_call(
        functools.partial(_norm_mod_kernel, g_row=g_row, shift_row=shift_row, scale_row=scale_row),
        grid_spec=pltpu.PrefetchScalarGridSpec(
            num_scalar_prefetch=1, grid=(lay.nblk,),
            in_specs=[pl.BlockSpec((tb, d), lambda i, mt: (i, 0)),
                      pl.BlockSpec((4, d), lambda i, mt: (0, 0)),
                      pl.BlockSpec((None, 6, d), lambda i, mt: (mt[0, i], 0, 0))],
            out_specs=pl.BlockSpec((tb, d), lambda i, mt: (i, 0))),
        out_shape=jax.ShapeDtypeStruct((m, d), BF16),
        compiler_params=_cparams("parallel"),
        name="norm_mod",
    )(lay.meta, x, g4, mods)


def _resid_norm_kernel(meta_ref, x_ref, mix_ref, g_ref, mod_ref, *rest, n_experts):
    if n_experts:
        rw_ref, xo_ref, h_ref, route_ref = rest
    else:
        xo_ref, h_ref = rest
    x = x_ref[...] + mod_ref[2:3, :] * _rms(mix_ref[...], g_ref[1:2, :])
    xo_ref[...] = x
    h = _modulated(x, g_ref[2:3, :], mod_ref, 3, 4)
    h_ref[...] = h.astype(h_ref.dtype)
    if n_experts:
        route_ref[...] = _router_route(h, rw_ref, n_experts)


def _resid_norm(lay, x, mix, g4, mods, router=None, n_experts=0):
    m, d = x.shape
    tb = lay.tb
    row = lambda i, mt: (i, 0)
    in_specs = [pl.BlockSpec((tb, d), row), pl.BlockSpec((tb, d), row),
                pl.BlockSpec((4, d), lambda i, mt: (0, 0)),
                pl.BlockSpec((None, 6, d), lambda i, mt: (mt[0, i], 0, 0))]
    out_specs = [pl.BlockSpec((tb, d), row), pl.BlockSpec((tb, d), row)]
    out_shape = [jax.ShapeDtypeStruct((m, d), F32), jax.ShapeDtypeStruct((m, d), F32 if n_experts else BF16)]
    args = [lay.meta, x, mix, g4, mods]
    if n_experts:
        in_specs.append(pl.BlockSpec((d, LANES), lambda i, mt: (0, 0)))
        out_specs.append(pl.BlockSpec((tb, LANES), row))
        out_shape.append(jax.ShapeDtypeStruct((m, LANES), F32))
        args.append(router)
    return pl.pallas_call(
        functools.partial(_resid_norm_kernel, n_experts=n_experts),
        grid_spec=pltpu.PrefetchScalarGridSpec(
            num_scalar_prefetch=1, grid=(lay.nblk,), in_specs=in_specs, out_specs=out_specs),
        out_shape=out_shape,
        compiler_params=_cparams("parallel"),
        name="resid_norm",
    )(*args)


def _resid_kernel(meta_ref, x_ref, f_ref, g_ref, mod_ref, xo_ref):
    xo_ref[...] = x_ref[...] + mod_ref[5:6, :] * _rms(f_ref[...], g_ref[3:4, :])


def _resid(lay, x, f, g4, mods):
    m, d = x.shape
    tb = lay.tb
    row = lambda i, mt: (i, 0)
    return pl.pallas_call(
        _resid_kernel,
        grid_spec=pltpu.PrefetchScalarGridSpec(
            num_scalar_prefetch=1, grid=(lay.nblk,),
            in_specs=[pl.BlockSpec((tb, d), row), pl.BlockSpec((tb, d), row),
                      pl.BlockSpec((4, d), lambda i, mt: (0, 0)),
                      pl.BlockSpec((None, 6, d), lambda i, mt: (mt[0, i], 0, 0))],
            out_specs=pl.BlockSpec((tb, d), row)),
        out_shape=jax.ShapeDtypeStruct((m, d), F32),
        compiler_params=_cparams("parallel"),
        name="resid",
    )(lay.meta, x, f, g4, mods)


def _mm_kernel(a_ref, w_ref, *rest, act, has_scale):
    if has_scale:
        s_ref, o_ref = rest
    else:
        (o_ref,) = rest
    a = a_ref[...]
    if act == "tanh":
        a = jnp.tanh(a)
    elif act == "sigmoid":
        a = jax.nn.sigmoid(a)
    out = _dot(a.astype(BF16), w_ref[...].astype(BF16))
    if has_scale:
        out = out * s_ref[...]
    o_ref[...] = out.astype(o_ref.dtype)


def _mm(a, w, lead=(), *, out_dtype=F32, act=None, a_col=0, n_groups=1, col_scale=None, tm=None, tn=None,
        a_single_buffer=False):
    m = a.shape[0]
    k, n = w.shape[-2:]
    tm = tm or _pick(m, (1024, 512, 256, 128, 64, 32, 16, 8))
    tn = tn or _pick(n, (512, 256, 128))
    nl = len(lead)
    if n_groups > 1:
        grid = (m // tm, n_groups, n // tn)
        a_spec = pl.BlockSpec((tm, k), lambda i, g, j: (i, g))
        w_spec = pl.BlockSpec((None,) * (nl + 1) + (k, tn), lambda i, g, j: lead + (g, 0, j))
        o_spec = pl.BlockSpec((tm, tn), lambda i, g, j: (i, g * (n // tn) + j))
        s_spec = pl.BlockSpec((1, tn), lambda i, g, j: (0, g * (n // tn) + j))
        sem = ("parallel", "arbitrary", "arbitrary")
    else:
        grid = (m // tm, n // tn)
        a_spec = pl.BlockSpec((tm, k), lambda i, j: (i, a_col),
                              **({"pipeline_mode": pl.Buffered(1)} if a_single_buffer else {}))
        w_spec = pl.BlockSpec((None,) * nl + (k, tn), lambda i, j: lead + (0, j))
        o_spec = pl.BlockSpec((tm, tn), lambda i, j: (i, j))
        s_spec = pl.BlockSpec((1, tn), lambda i, j: (0, j))
        sem = ("parallel", "arbitrary")
    in_specs, args = [a_spec, w_spec], [a, w]
    if col_scale is not None:
        in_specs.append(s_spec)
        args.append(col_scale)
    return pl.pallas_call(
        functools.partial(_mm_kernel, act=act, has_scale=col_scale is not None),
        grid=grid, in_specs=in_specs, out_specs=o_spec,
        out_shape=jax.ShapeDtypeStruct((m, n * n_groups), out_dtype),
        compiler_params=_cparams(*sem),
        name="mm",
    )(*args)


def _swiglu_kernel(a_ref, wg_ref, wu_ref, o_ref):
    a = a_ref[...]
    g = _dot(a, wg_ref[...].astype(BF16))
    u = _dot(a, wu_ref[...].astype(BF16))
    o_ref[...] = (_silu(g) * u).astype(o_ref.dtype)


def _swiglu_hidden(a, w_gu, lead):
    m, d = a.shape
    f = w_gu.shape[-1] // 2
    tm = _pick(m, (1024, 512, 256, 128, 64, 32, 16, 8))
    tn = _pick(f, (256, 128))
    nf = f // tn
    wlead = (None,) * len(lead)
    return pl.pallas_call(
        _swiglu_kernel,
        grid=(m // tm, nf),
        in_specs=[pl.BlockSpec((tm, d), lambda i, j: (i, 0)),
                  pl.BlockSpec(wlead + (d, tn), lambda i, j: lead + (0, j)),
                  pl.BlockSpec(wlead + (d, tn), lambda i, j: lead + (0, nf + j))],
        out_specs=pl.BlockSpec((tm, tn), lambda i, j: (i, j)),
        out_shape=jax.ShapeDtypeStruct((m, f), BF16),
        compiler_params=_cparams("parallel", "arbitrary"),
        name="swiglu",
    )(a, w_gu, w_gu)


MOE_ROW_TILE = 512
MOE_GATHER_ROWS = 256


def _moe_plan(route, n_experts, tm):
    m = route.shape[0]
    ids = route[:, :TOP_K].astype(jnp.int32).reshape(-1)
    onehot = (ids[:, None] == jnp.arange(n_experts, dtype=jnp.int32)[None, :]).astype(jnp.int32)
    csum = jnp.cumsum(onehot, axis=0)
    counts = csum[-1]
    rank = jnp.sum(onehot * csum, axis=1) - 1
    tiles_per = (counts + tm - 1) // tm
    tile_end = jnp.cumsum(tiles_per)
    row_start = (tile_end - tiles_per) * tm
    pos = jnp.sum(onehot * row_start[None, :], axis=1) + rank
    n_tiles = (TOP_K * m) // tm + n_experts
    row_token = jnp.zeros((n_tiles * tm,), jnp.int32).at[pos].set(jnp.arange(TOP_K * m, dtype=jnp.int32) // TOP_K)
    t = jnp.arange(n_tiles, dtype=jnp.int32)
    n_used = tile_end[-1]
    tile_expert = jnp.sum((t[:, None] >= tile_end[None, :]).astype(jnp.int32), axis=1)
    last_expert = jnp.sum((n_used - 1 >= tile_end).astype(jnp.int32))
    tile_expert = jnp.where(t < n_used, tile_expert, last_expert)
    return row_token, tile_expert, n_used.reshape(1), pos.reshape(m, TOP_K)


def _moe_gather_kernel(nu_ref, tok_ref, tok_next_ref, h_hbm, o_ref, buf, sem, *, tiles_per_row_tile):
    i = pl.program_id(0)
    tg = buf.shape[1]
    n_active = nu_ref[0] * tiles_per_row_tile
    slot = i % 2

    def copy(idx_ref, k, s):
        return pltpu.make_async_copy(h_hbm.at[pl.ds(idx_ref[0, k], 1), :], buf.at[s, pl.ds(k, 1), :], sem.at[s])

    def start_tile(idx_ref, s):
        def start(k, carry):
            copy(idx_ref, k, s).start()
            return carry

        lax.fori_loop(0, tg, start, 0)

    @pl.when(i == 0)
    def _():
        start_tile(tok_ref, 0)

    @pl.when(i + 1 < n_active)
    def _():
        start_tile(tok_next_ref, 1 - slot)

    @pl.when(i < n_active)
    def _():
        def wait(k, carry):
            copy(tok_ref, k, slot).wait()
            return carry

        lax.fori_loop(0, tg, wait, 0)
        o_ref[...] = buf[slot].astype(o_ref.dtype)

    @pl.when(i >= n_active)
    def _():
        o_ref[...] = jnp.zeros_like(o_ref)


def _moe_gather(h, row_token, n_used, tm):
    m, d = h.shape
    rows = row_token.shape[0]
    tg = _pick(tm, (MOE_GATHER_ROWS, 128, 64, 32, 16, 8))
    nt = rows // tg
    tok = row_token.reshape(nt, 1, tg)
    return pl.pallas_call(
        functools.partial(_moe_gather_kernel, tiles_per_row_tile=tm // tg),
        grid_spec=pltpu.PrefetchScalarGridSpec(
            num_scalar_prefetch=1, grid=(nt,),
            in_specs=[pl.BlockSpec((None, 1, tg), lambda i, nu: (i, 0, 0), memory_space=pltpu.SMEM),
                      pl.BlockSpec((None, 1, tg), lambda i, nu: (jnp.minimum(i + 1, nt - 1), 0, 0),
                                   memory_space=pltpu.SMEM),
                      pl.BlockSpec(memory_space=pl.ANY)],
            out_specs=pl.BlockSpec((tg, d), lambda i, nu: (i, 0)),
            scratch_shapes=[pltpu.VMEM((2, tg, d), F32), pltpu.SemaphoreType.DMA((2,))]),
        out_shape=jax.ShapeDtypeStruct((rows, d), BF16),
        compiler_params=_cparams("arbitrary"),
        name="moe_gather",
    )(n_used, tok, tok, h)


def _moe_swiglu_kernel(te_ref, nu_ref, a_ref, wg_ref, wu_ref, o_ref):
    @pl.when(pl.program_id(0) < nu_ref[0])
    def _():
        a = a_ref[...]
        g = _dot(a, wg_ref[...].astype(BF16))
        u = _dot(a, wu_ref[...].astype(BF16))
        o_ref[...] = (_silu(g) * u).astype(o_ref.dtype)

    @pl.when(pl.program_id(0) >= nu_ref[0])
    def _():
        o_ref[...] = jnp.zeros_like(o_ref)


def _moe_swiglu(a, w_gu, fi, tile_expert, n_used, tm):
    rows, d = a.shape
    f = w_gu.shape[-1] // 2
    tn = _pick(f, (256, 128))
    nf = f // tn
    return pl.pallas_call(
        _moe_swiglu_kernel,
        grid_spec=pltpu.PrefetchScalarGridSpec(
            num_scalar_prefetch=2, grid=(rows // tm, nf),
            in_specs=[pl.BlockSpec((tm, d), lambda t, j, te, nu: (t, 0)),
                      pl.BlockSpec((None, None, d, tn), lambda t, j, te, nu: (fi, te[t], 0, j)),
                      pl.BlockSpec((None, None, d, tn), lambda t, j, te, nu: (fi, te[t], 0, nf + j))],
            out_specs=pl.BlockSpec((tm, tn), lambda t, j, te, nu: (t, j))),
        out_shape=jax.ShapeDtypeStruct((rows, f), BF16),
        compiler_params=_cparams("parallel", "arbitrary"),
        name="moe_swiglu",
    )(tile_expert, n_used, a, w_gu, w_gu)


def _moe_down_kernel(te_ref, nu_ref, a_ref, w_ref, o_ref):
    @pl.when(pl.program_id(0) < nu_ref[0])
    def _():
        o_ref[...] = _dot(a_ref[...], w_ref[...].astype(BF16))

    @pl.when(pl.program_id(0) >= nu_ref[0])
    def _():
        o_ref[...] = jnp.zeros_like(o_ref)


def _moe_down(a, w_down, fi, tile_expert, n_used, tm):
    rows, f = a.shape
    d = w_down.shape[-1]
    tn = _pick(d, (512, 256, 128))
    return pl.pallas_call(
        _moe_down_kernel,
        grid_spec=pltpu.PrefetchScalarGridSpec(
            num_scalar_prefetch=2, grid=(rows // tm, d // tn),
            in_specs=[pl.BlockSpec((tm, f), lambda t, j, te, nu: (t, 0)),
                      pl.BlockSpec((None, None, f, tn), lambda t, j, te, nu: (fi, te[t], 0, j))],
            out_specs=pl.BlockSpec((tm, tn), lambda t, j, te, nu: (t, j))),
        out_shape=jax.ShapeDtypeStruct((rows, d), F32),
        compiler_params=_cparams("parallel", "arbitrary"),
        name="moe_down",
    )(tile_expert, n_used, a, w_down)


def _resid_moe_kernel(meta_ref, pos_ref, pos_next_ref, x_ref, route_ref, g_ref, mod_ref, y_hbm, xo_ref, buf, sem):
    i = pl.program_id(0)
    tb = x_ref.shape[0]
    slot = i % 2

    def copy(idx_ref, k, e, s):
        return pltpu.make_async_copy(y_hbm.at[pl.ds(idx_ref[0, TOP_K * k + e], 1), :],
                                     buf.at[s, e, pl.ds(k, 1), :], sem.at[s])

    def start_tile(idx_ref, s):
        def start(k, carry):
            for e in range(TOP_K):
                copy(idx_ref, k, e, s).start()
            return carry

        lax.fori_loop(0, tb, start, 0)

    @pl.when(i == 0)
    def _():
        start_tile(pos_ref, 0)

    @pl.when(i + 1 < pl.num_programs(0))
    def _():
        start_tile(pos_next_ref, 1 - slot)

    def wait(k, carry):
        for e in range(TOP_K):
            copy(pos_ref, k, e, slot).wait()
        return carry

    lax.fori_loop(0, tb, wait, 0)
    route = route_ref[...]
    f = route[:, 2:3] * buf[slot, 0] + route[:, 3:4] * buf[slot, 1]
    xo_ref[...] = x_ref[...] + mod_ref[5:6, :] * _rms(f, g_ref[3:4, :])


def _resid_moe(lay, x, y_sorted, pos, route, g4, mods):
    m, d = x.shape
    tb = lay.tb
    row = lambda i, mt: (i, 0)
    pos3 = pos.reshape(lay.nblk, 1, TOP_K * tb)
    return pl.pallas_call(
        _resid_moe_kernel,
        grid_spec=pltpu.PrefetchScalarGridSpec(
            num_scalar_prefetch=1, grid=(lay.nblk,),
            in_specs=[pl.BlockSpec((None, 1, TOP_K * tb), lambda i, mt: (i, 0, 0), memory_space=pltpu.SMEM),
                      pl.BlockSpec((None, 1, TOP_K * tb), lambda i, mt: (jnp.minimum(i + 1, lay.nblk - 1), 0, 0),
                                   memory_space=pltpu.SMEM),
                      pl.BlockSpec((tb, d), row), pl.BlockSpec((tb, LANES), row),
                      pl.BlockSpec((4, d), lambda i, mt: (0, 0)),
                      pl.BlockSpec((None, 6, d), lambda i, mt: (mt[0, i], 0, 0)),
                      pl.BlockSpec(memory_space=pl.ANY)],
            out_specs=pl.BlockSpec((tb, d), row),
            scratch_shapes=[pltpu.VMEM((2, TOP_K, tb, d), F32), pltpu.SemaphoreType.DMA((2,))]),
        out_shape=jax.ShapeDtypeStruct((m, d), F32),
        compiler_params=_cparams("arbitrary"),
        name="resid_moe",
    )(lay.meta, pos3, pos3, x, route, g4, mods, y_sorted)


def _softmax_rows(s):
    e = jnp.exp(s - jnp.max(s, axis=-1, keepdims=True))
    return e * (1.0 / jnp.sum(e, axis=-1, keepdims=True))


def _attn_ctx_kernel(q_ref, k_ref, v_ref, _, o_ref, *, heads, scale):
    for h in range(heads):
        sl = slice(h * ATTN_HEAD_DIM, (h + 1) * ATTN_HEAD_DIM)
        s = _dot_nt(q_ref[:, sl].astype(BF16), k_ref[:, sl].astype(BF16)) * scale
        p = _softmax_rows(s).astype(BF16)
        o_ref[:, sl] = _dot(p, v_ref[:, sl].astype(BF16)).astype(o_ref.dtype)


def _attn_context(qkv, n_seq, seq_len, d, o_buf):
    n_heads = d // ATTN_HEAD_DIM
    hb = _pick(n_heads, (4, 2, 1))
    wb = hb * ATTN_HEAD_DIM
    ncb = d // wb
    return pl.pallas_call(
        functools.partial(_attn_ctx_kernel, heads=hb, scale=ATTN_HEAD_DIM ** -0.5),
        grid=(n_seq, ncb),
        in_specs=[pl.BlockSpec((seq_len, wb), lambda b, h: (b, h)),
                  pl.BlockSpec((seq_len, wb), lambda b, h: (b, ncb + h)),
                  pl.BlockSpec((seq_len, wb), lambda b, h: (b, 2 * ncb + h)),
                  pl.BlockSpec(memory_space=pl.ANY)],
        out_specs=pl.BlockSpec((seq_len, wb), lambda b, h: (b, h)),
        out_shape=jax.ShapeDtypeStruct(o_buf.shape, o_buf.dtype),
        input_output_aliases={3: 0},
        compiler_params=_cparams("parallel", "parallel"),
        name="attn_context",
    )(qkv, qkv, qkv, o_buf)


def _attn_lat_kernel(q_ref, k_ref, v_ref, ck_ref, cv_ref, tb_ref, _, o_ref, *, rows, wr, scale):
    kb = k_ref[...].astype(BF16)
    vb = v_ref[...].astype(BF16)
    ckb = ck_ref[...].astype(BF16)
    cvb = cv_ref[...].astype(BF16)
    biases = {}
    for r in range(rows):
        rs = min(max(r - wr // 2, 0), rows - wr)
        d0 = rs - r + MAX_WIN_ROWS - 1
        if d0 not in biases:
            biases[d0] = jnp.concatenate([tb_ref[d0 + a] for a in range(wr)], axis=1)
        q = q_ref[r * GRID_W:(r + 1) * GRID_W, :].astype(BF16)
        kl = kb[rs * GRID_W:(rs + wr) * GRID_W]
        vl = vb[rs * GRID_W:(rs + wr) * GRID_W]
        s_loc = _dot_nt(q, kl) * scale + biases[d0]
        s_ctx = _dot_nt(q, ckb) * scale
        mx = jnp.maximum(jnp.max(s_loc, axis=-1, keepdims=True), jnp.max(s_ctx, axis=-1, keepdims=True))
        e_loc = jnp.exp(s_loc - mx)
        e_ctx = jnp.exp(s_ctx - mx)
        inv = 1.0 / (jnp.sum(e_loc, axis=-1, keepdims=True) + jnp.sum(e_ctx, axis=-1, keepdims=True))
        o = _dot((e_loc * inv).astype(BF16), vl) + _dot((e_ctx * inv).astype(BF16), cvb)
        o_ref[r * GRID_W:(r + 1) * GRID_W, :] = o.astype(o_ref.dtype)


def _rel_bias_table(rpb):
    qc = np.arange(GRID_W)[:, None]
    kc = np.arange(GRID_W)[None, :]
    ws = np.clip(qc - WIN_COLS // 2, 0, GRID_W - WIN_COLS)
    in_win = (kc >= ws) & (kc < ws + WIN_COLS)
    dcol = np.clip(kc - qc, 1 - WIN_COLS, WIN_COLS - 1) + WIN_COLS - 1
    return jnp.where(in_win[None, None], rpb.astype(F32)[:, :, dcol], NEG_INF)


def _attn_latent(qkv, cache_k, cache_v, mi, rpb, row_block0, n_seq, seq_len, d, o_buf):
    n_heads = d // ATTN_HEAD_DIM
    rows = seq_len // GRID_W
    wr = min(MAX_WIN_ROWS, rows)
    past = cache_k.shape[2]
    ck = cache_k.reshape(cache_k.shape[0], cache_k.shape[1], past, d)
    cv = cache_v.reshape(cache_v.shape[0], cache_v.shape[1], past, d)
    table = _rel_bias_table(rpb)
    hd = ATTN_HEAD_DIM
    return pl.pallas_call(
        functools.partial(_attn_lat_kernel, rows=rows, wr=wr, scale=ATTN_HEAD_DIM ** -0.5),
        grid=(n_seq, n_heads),
        in_specs=[pl.BlockSpec((seq_len, hd), lambda b, h: (row_block0 + b, h)),
                  pl.BlockSpec((seq_len, hd), lambda b, h: (row_block0 + b, n_heads + h)),
                  pl.BlockSpec((seq_len, hd), lambda b, h: (row_block0 + b, 2 * n_heads + h)),
                  pl.BlockSpec((None, None, past, hd), lambda b, h: (b, mi, 0, h)),
                  pl.BlockSpec((None, None, past, hd), lambda b, h: (b, mi, 0, h)),
                  pl.BlockSpec((None, 2 * MAX_WIN_ROWS - 1, GRID_W, GRID_W), lambda b, h: (h, 0, 0, 0)),
                  pl.BlockSpec(memory_space=pl.ANY)],
        out_specs=pl.BlockSpec((seq_len, hd), lambda b, h: (row_block0 + b, h)),
        out_shape=jax.ShapeDtypeStruct(o_buf.shape, o_buf.dtype),
        input_output_aliases={6: 0},
        compiler_params=_cparams("parallel", "parallel"),
        name="attn_latent",
    )(qkv, qkv, qkv, ck, cv, table, o_buf)


def _halo_rows(meta_ref, h_prev_blk, h_next_blk):
    i = pl.program_id(0)
    hp = jnp.where(meta_ref[1, i] == 1, 0.0, h_prev_blk[POOL_HALO - 1:POOL_HALO, :])
    hn = jnp.where(meta_ref[2, i] == 1, 0.0, h_next_blk[0:1, :])
    return hp, hn


def _rwkv_prep_kernel(meta_ref, x_ref, xp_ref, xn_ref, g_ref, mod_ref, mu_ref, *o_refs):
    g = g_ref[0:1, :]
    h = _modulated(x_ref[...], g, mod_ref, 0, 1)
    hp, hn = _halo_rows(meta_ref, _modulated(xp_ref[...], g, mod_ref, 0, 1),
                        _modulated(xn_ref[...], g, mod_ref, 0, 1))
    tb = h.shape[0]
    row = lax.broadcasted_iota(jnp.int32, h.shape, 0)
    prev = jnp.where(row == 0, hp, pltpu.roll(h, 1, 0))
    nxt = jnp.where(row == tb - 1, hn, pltpu.roll(h, tb - 1, 0))
    xx = 0.5 * (prev + nxt) - h
    for n, o_ref in enumerate(o_refs):
        o_ref[...] = (h + xx * mu_ref[n:n + 1, :]).astype(o_ref.dtype)


def _halo_specs(lay, d):
    tb = lay.tb
    per = tb // POOL_HALO
    last = lay.nblk * per - 1
    return [pl.BlockSpec((tb, d), lambda i, mt: (i, 0)),
            pl.BlockSpec((POOL_HALO, d), lambda i, mt: (jnp.maximum(i * per - 1, 0), 0)),
            pl.BlockSpec((POOL_HALO, d), lambda i, mt: (jnp.minimum((i + 1) * per, last), 0))]


def _rwkv_prep(lay, x, g4, mods, mu):
    m, d = x.shape
    tb = lay.tb
    return pl.pallas_call(
        _rwkv_prep_kernel,
        grid_spec=pltpu.PrefetchScalarGridSpec(
            num_scalar_prefetch=1, grid=(lay.nblk,),
            in_specs=_halo_specs(lay, d) + [
                pl.BlockSpec((4, d), lambda i, mt: (0, 0)),
                pl.BlockSpec((None, 6, d), lambda i, mt: (mt[0, i], 0, 0)),
                pl.BlockSpec((6, d), lambda i, mt: (0, 0))],
            out_specs=[pl.BlockSpec((tb, d), lambda i, mt: (i, 0))] * 6),
        out_shape=[jax.ShapeDtypeStruct((m, d), BF16)] * 6,
        compiler_params=_cparams("parallel"),
        name="rwkv_prep",
    )(lay.meta, x, x, x, g4, mods, mu)


def _split(x):
    hi = x.astype(BF16)
    return hi, (x - hi.astype(F32)).astype(BF16)


def _dot3(a, b):
    (ah, al), (bh, bl) = a, b
    return _dot(jnp.concatenate([ah, ah, al], axis=1), jnp.concatenate([bh, bl, bh], axis=0))


def _dot3_nt(a, b):
    (ah, al), (bh, bl) = a, b
    return _dot_nt(jnp.concatenate([ah, ah, al], axis=1), jnp.concatenate([bh, bl, bh], axis=1))


def _rwkv_scan_kernel(*refs, seq_len, has_s0, pairs):
    if has_s0:
        (r_ref, k_ref, v_ref, w0_ref, w1_ref, a0_ref, a1_ref, g_ref, pv_ref, s0_ref, _, y_ref, yf_ref, yb_ref) = refs
        sf_ref = None
    else:
        (r_ref, k_ref, v_ref, w0_ref, w1_ref, a0_ref, a1_ref, g_ref, pv_ref, _, y_ref, sf_ref, yf_ref, yb_ref) = refs
        s0_ref = None
    c = RWKV_CHUNK
    n = RWKV_HEAD_DIM
    nchunks = seq_len // c
    wpre_refs = (w0_ref, w1_ref)
    apre_refs = (a0_ref, a1_ref)
    def iota(shape, dim):
        return lax.broadcasted_iota(jnp.int32, shape, dim)

    head0 = iota((c, LANES), 1) < n
    seg = (iota((LANES, LANES), 0) // n == iota((LANES, LANES), 1) // n)
    seg_f = seg.astype(F32)
    seg_b = seg_f.astype(BF16)
    tt = iota((c, 2 * c), 0)
    ss = iota((c, 2 * c), 1) % c
    lo_half = iota((c, 2 * c), 1) < c
    strict = (ss < tt, ss > tt)
    incl = (ss <= tt, ss >= tt)
    t2 = iota((c, c), 0)
    s2 = iota((c, c), 1)
    tri = ((s2 <= t2).astype(F32).astype(BF16), (s2 >= t2).astype(F32).astype(BF16))

    def seg_sum(x):
        hi, lo = _split(x)
        return _dot(jnp.concatenate([hi, lo], axis=1), jnp.concatenate([seg_b, seg_b], axis=0))

    head0_b = head0.astype(F32).astype(BF16)
    head1_b = (1.0 - head0.astype(F32)).astype(BF16)
    lo_half_b = lo_half.astype(F32).astype(BF16)
    hi_half_b = (1.0 - lo_half.astype(F32)).astype(BF16)

    def stack2(x):
        return jnp.concatenate([x * head0_b, x * head1_b], axis=0)

    def stack2_parts(parts):
        return tuple(stack2(p) for p in parts)

    def blockdiag(x):
        return jnp.concatenate([x * lo_half_b, x * hi_half_b], axis=0)

    def cat0(a, b):
        return tuple(jnp.concatenate([x, y], axis=0) for x, y in zip(a, b))

    def features(rows, lanes, d):
        k = k_ref[rows, lanes]
        kk = k * pv_ref[0:1, lanes]
        kk = kk / jnp.maximum(jnp.sqrt(seg_sum(kk * kk)), 1e-12)
        wx = -(pv_ref[5 + d:6 + d, lanes] + wpre_refs[d][rows, lanes])
        w_log = -(jnp.maximum(wx, 0.0) + jnp.log(1.0 + jnp.exp(-jnp.abs(wx)))) - 0.5
        logw = -jnp.exp(w_log)
        a = jax.nn.sigmoid(pv_ref[7 + d:8 + d, lanes] + apre_refs[d][rows, lanes])
        kd = k * (1.0 + (a - 1.0) * pv_ref[1:2, lanes])
        return kk, kd, kk * a, logw

    def chunk(rows, lanes, d, s_bd):
        r = r_ref[rows, lanes]
        v = v_ref[rows, lanes]
        kk, kd, bb, logw = features(rows, lanes, d)
        yield
        l1 = logw.astype(BF16)
        l2 = (logw - l1.astype(F32)).astype(BF16)
        l3 = (logw - l1.astype(F32) - l2.astype(F32)).astype(BF16)
        cum = _dot(jnp.concatenate([tri[d]] * 3, axis=1), jnp.concatenate([l1, l2, l3], axis=0))
        yield
        total = cum[c - 1:c, :] if d == 0 else cum[0:1, :]
        e_inv = jnp.exp(-cum)
        e_rest = jnp.exp(total - cum)
        at = _split(kk * jnp.exp(cum - logw))
        rt = (r * jnp.exp(cum)).astype(BF16)
        kb = cat0(stack2_parts(_split(kd * e_inv)), stack2_parts(_split(bb * e_inv)))
        s_parts = _split(s_bd)
        kbs = cat0(kb, s_parts)
        sa = _dot3_nt(at, kbs)
        sr = _dot_nt(rt, kbs[0])
        yield
        zero = jnp.zeros((c, 2 * c), F32)
        m_cat = jnp.where(strict[d], sa[:, :2 * c], zero)
        l_cat = jnp.where(strict[d], sa[:, 2 * c:4 * c], zero)
        rk_cat = jnp.where(incl[d], sr[:, :2 * c], zero)
        rb_cat = jnp.where(incl[d], sr[:, 2 * c:4 * c], zero)
        y0 = sr[:, 4 * c:]
        v_st = stack2_parts(_split(v))
        x = sa[:, 4 * c:] + _dot3(_split(m_cat), v_st)
        yield
        p = -l_cat
        n_stage = c.bit_length() - 1
        for stage in range(n_stage):
            pp = _split(p)
            xs = stack2_parts(_split(x))
            if stage + 1 < n_stage:
                px = _dot3(pp, tuple(jnp.concatenate([xi, blockdiag(pi)], axis=1) for xi, pi in zip(xs, pp)))
                x = x + px[:, :LANES]
                p = px[:, LANES:]
            else:
                x = x + _dot3(pp, xs)
            yield
        u = x
        u_parts = _split(u)
        y = y0 + _dot(
            jnp.concatenate([rk_cat, rb_cat], axis=1).astype(BF16),
            jnp.concatenate([v_st[0], -stack2(u_parts[0])], axis=0))
        vu_t = _split(jnp.concatenate([v, -u], axis=0).T)
        kb_rest = cat0(_split(kd * e_rest), _split(bb * e_rest))
        s_new = s_bd * jnp.exp(total) + seg_f * _dot3(vu_t, kb_rest)
        return y, s_new

    def run_lockstep(gens):
        results = [None] * len(gens)
        active = list(range(len(gens)))
        while active:
            for i in list(active):
                try:
                    next(gens[i])
                except StopIteration as stop:
                    results[i] = stop.value
                    active.remove(i)
        return results

    chains = [(pi, d) for pi in range(pairs) for d in range(2)]
    if has_s0:
        z = jnp.zeros((n, n), F32)
        s_init = tuple(
            jnp.concatenate([jnp.concatenate([s0_ref[d, 2 * pi], z], axis=1),
                             jnp.concatenate([z, s0_ref[d, 2 * pi + 1]], axis=1)], axis=0) for pi, d in chains)
    else:
        s_init = tuple(jnp.zeros((LANES, LANES), F32) for _ in chains)

    def body(ci, carry):
        rows_fb = (pl.ds(pl.multiple_of(ci * c, c), c), pl.ds(pl.multiple_of((nchunks - 1 - ci) * c, c), c))
        lanes = [slice(pi * LANES, (pi + 1) * LANES) for pi, _ in chains]
        results = run_lockstep([chunk(rows_fb[d], ln, d, s_bd)
                                for (_, d), ln, s_bd in zip(chains, lanes, carry)])
        for (_, d), ln, (y, _) in zip(chains, lanes, results):
            (yf_ref, yb_ref)[d][rows_fb[d], ln] = y
        return tuple(s_bd for _, s_bd in results)

    s_fin = lax.fori_loop(0, nchunks, body, s_init)
    if sf_ref is not None:
        for (pi, d), s_bd in zip(chains, s_fin):
            sf_ref[d, 2 * pi] = s_bd[:n, :n]
            sf_ref[d, 2 * pi + 1] = s_bd[n:, n:]

    fr = 2 * c if seq_len % (2 * c) == 0 else c

    def finish_rows(rows, lanes):
        y = yf_ref[rows, lanes] + yb_ref[rows, lanes]
        a_sum = sum(jax.nn.sigmoid(pv_ref[7 + d:8 + d, lanes] + apre_refs[d][rows, lanes]) for d in range(2))
        kd_sum = k_ref[rows, lanes] * (2.0 + (a_sum - 2.0) * pv_ref[1:2, lanes])
        sums = seg_sum(jnp.concatenate([y, r_ref[rows, lanes] * kd_sum * pv_ref[2:3, lanes]], axis=0))
        yield
        yc = y - sums[:fr] * (1.0 / n)
        var = seg_sum(yc * yc) * (1.0 / n)
        yield
        y = yc * lax.rsqrt(var + LNX_EPS) * pv_ref[3:4, lanes] + pv_ref[4:5, lanes] + sums[fr:] * v_ref[rows, lanes]
        y_ref[rows, lanes] = (y * g_ref[rows, lanes]).astype(y_ref.dtype)

    def finish(ci, carry):
        rows = pl.ds(pl.multiple_of(ci * fr, fr), fr)
        run_lockstep([finish_rows(rows, slice(pi * LANES, (pi + 1) * LANES)) for pi in range(pairs)])
        return carry

    lax.fori_loop(0, seq_len // fr, finish, 0)


def _rwkv_scan(r, k, v, wpre, apre, g, pvec, *, row_block0, n_seq, seq_len, s0=None, mi=0, y_buf=None):
    m, d = r.shape
    n = RWKV_HEAD_DIM
    n_f32_blocks = 2 * 8 + 2 + 1
    pairs = next(p for p in (4, 2, 1)
                 if (d // LANES) % p == 0 and n_f32_blocks * seq_len * p * LANES * 4 <= RWKV_VMEM_BUDGET_BYTES)
    wb = pairs * LANES
    nblk = d // wb
    blk = pl.BlockSpec((seq_len, wb), lambda b, p: (row_block0 + b, p))
    in_specs = [blk] * 8 + [pl.BlockSpec((16, wb), lambda b, p: (0, p))]
    args = [r, k, v, wpre[0], wpre[1], apre[0], apre[1], g, pvec]
    y_shape = jax.ShapeDtypeStruct((m, d), BF16)
    if s0 is not None:
        in_specs.append(pl.BlockSpec((None, None, 2, 2 * pairs, n, n), lambda b, p: (b, mi, 0, p, 0, 0)))
        args.append(s0)
    in_specs.append(pl.BlockSpec(memory_space=pl.ANY))
    args.append(y_buf)
    aliases = {len(args) - 1: 0}
    if s0 is not None:
        out_specs, out_shape = blk, y_shape
    else:
        out_specs = [blk, pl.BlockSpec((None, 2, 2 * pairs, n, n), lambda b, p: (b, 0, p, 0, 0))]
        out_shape = [y_shape, jax.ShapeDtypeStruct((n_seq, 2, d // n, n, n), F32)]
    return pl.pallas_call(
        functools.partial(_rwkv_scan_kernel, seq_len=seq_len, has_s0=s0 is not None, pairs=pairs),
        grid=(n_seq, nblk),
        in_specs=in_specs, out_specs=out_specs, out_shape=out_shape, input_output_aliases=aliases,
        scratch_shapes=[pltpu.VMEM((seq_len, wb), F32), pltpu.VMEM((seq_len, wb), F32)],
        compiler_params=_cparams("parallel", "parallel"),
        name="rwkv_scan",
    )(*args)


def _pool_prep_kernel(meta_ref, x_ref, xp_ref, xn_ref, g_ref, mod_ref, o_ref):
    i = pl.program_id(0)
    g = g_ref[0:1, :]
    h = _modulated(x_ref[...], g, mod_ref, 0, 1)
    hp = jnp.where(meta_ref[1, i] == 1, 0.0, _modulated(xp_ref[...], g, mod_ref, 0, 1))
    hn = jnp.where(meta_ref[2, i] == 1, 0.0, _modulated(xn_ref[...], g, mod_ref, 0, 1))
    tb, d = h.shape
    pd = d // len(POOL_WINDOWS)
    ext = jnp.concatenate([hp, h, hn], axis=0)
    ne = tb + 2 * POOL_HALO
    row = lax.broadcasted_iota(jnp.int32, (tb, pd), 0)
    at_first = (meta_ref[1, i] == 1).astype(jnp.int32)
    at_last = (meta_ref[2, i] == 1).astype(jnp.int32)
    acc = ext[:, 0:d] + pltpu.roll(ext, 1, 0)
    half = 1
    for gi, win in enumerate(POOL_WINDOWS):
        if gi > 0:
            sub = acc[:, pd:]
            acc = pltpu.roll(sub, half, 0) + pltpu.roll(sub, ne - half, 0)
            half *= 2
        wsum = acc[POOL_HALO:POOL_HALO + tb, 0:pd]
        missing = (at_first * jnp.maximum(win // 2 - row, 0) + at_last * jnp.maximum(row + win // 2 - tb, 0))
        cnt = (win - missing).astype(F32)
        o_ref[:, gi * pd:(gi + 1) * pd] = (wsum / cnt - h[:, gi * pd:(gi + 1) * pd]).astype(o_ref.dtype)


def _pool_prep(lay, x, g4, mods):
    m, d = x.shape
    tb = lay.tb
    return pl.pallas_call(
        _pool_prep_kernel,
        grid_spec=pltpu.PrefetchScalarGridSpec(
            num_scalar_prefetch=1, grid=(lay.nblk,),
            in_specs=_halo_specs(lay, d) + [
                pl.BlockSpec((4, d), lambda i, mt: (0, 0)),
                pl.BlockSpec((None, 6, d), lambda i, mt: (mt[0, i], 0, 0))],
            out_specs=pl.BlockSpec((tb, d), lambda i, mt: (i, 0))),
        out_shape=jax.ShapeDtypeStruct((m, d), BF16),
        compiler_params=_cparams("parallel"),
        name="pool_prep",
    )(lay.meta, x, x, x, g4, mods)


def kernel(x_prompt, x_sample, cache_k, cache_v, state_rwkv, c, c_ctx, ada_w, ada_b, norm_g,
           attn_w_qkv, attn_w_o, attn_rpb, rwkv_mu, rwkv_w_r, rwkv_w_k, rwkv_w_v, rwkv_w_o,
           rwkv_w0, rwkv_w1, rwkv_w2, rwkv_a0, rwkv_a1, rwkv_a2, rwkv_g1, rwkv_g2, rwkv_k_k, rwkv_k_a,
           rwkv_r_k, rwkv_lnx_w, rwkv_lnx_b, pool_w, pool_scale, ffn_w_gu, ffn_w_down,
           moe_router, moe_w_gu, moe_w_down):
    nb, seq, d = x_prompt.shape
    ndb, dseq, _ = x_sample.shape
    depth = ada_w.shape[0]
    n_experts = moe_router.shape[-1]
    lay = _Layout(nb, seq, ndb, dseq, _pick(np.gcd(seq, dseq), (TOK_BLOCK, 128, 64, 32, 16)))
    lay_prep = _Layout(nb, seq, ndb, dseq, _pick(np.gcd(seq, dseq), (TOK_BLOCK // 2, 64, 32, 16)))
    assert lay.mp % dseq == 0
    lat_block0 = lay.mp // dseq

    x = jnp.concatenate([x_prompt.reshape(lay.mp, d), x_sample.reshape(lay.ms, d)], axis=0)
    cond8 = jnp.zeros((8, d), F32).at[0].set(c_ctx).at[1:1 + ndb].set(c)
    mods_all = _ada_all(cond8, ada_w, ada_b).reshape(depth, 8, 6, d)

    new_k, new_v, new_s = [], [], []
    for i in range(depth):
        kind, mi, fi = i % 3, i // 3, i // 2
        mods = mods_all[i]
        g4 = norm_g[i]
        if kind == 0:
            h = _norm_mod(lay, x, g4, mods, g_row=0, shift_row=0, scale_row=1)
            qkv = _mm(h, attn_w_qkv, (mi,))
            o = _attn_context(qkv, nb, seq, d, jnp.zeros((lay.m, d), BF16))
            o = _attn_latent(qkv, cache_k, cache_v, mi, attn_rpb[mi], lat_block0, ndb, dseq, d, o)
            mix = _mm(o, attn_w_o, (mi,))
            heads = d // ATTN_HEAD_DIM
            new_k.append(qkv[:lay.mp, d:2 * d].reshape(nb, seq, heads, ATTN_HEAD_DIM))
            new_v.append(qkv[:lay.mp, 2 * d:].reshape(nb, seq, heads, ATTN_HEAD_DIM))
        elif kind == 1:
            xr, xw, xk, xv, xa, xg = _rwkv_prep(lay_prep, x, g4, mods, rwkv_mu[mi])
            r = _mm(xr, rwkv_w_r, (mi,))
            k = _mm(xk, rwkv_w_k, (mi,))
            v = _mm(xv, rwkv_w_v, (mi,))
            lw, la = rwkv_w1.shape[-1], rwkv_a1.shape[-1]
            w1cat = jnp.moveaxis(rwkv_w1[mi], 0, 1).reshape(d, 2 * lw)
            a1cat = jnp.moveaxis(rwkv_a1[mi], 0, 1).reshape(d, 2 * la)
            lg = rwkv_g1.shape[-1]
            lgp = -(-lg // LANES) * LANES
            g1p = jnp.pad(rwkv_g1[mi], ((0, 0), (0, lgp - lg)))
            g2p = jnp.pad(rwkv_g2[mi], ((0, lgp - lg), (0, 0)))
            tw = _mm(xw, w1cat)
            ta = _mm(xa, a1cat)
            tg = _mm(xg, g1p)
            wpre = [_mm(tw, rwkv_w2, (mi, dd), act="tanh", a_col=dd) for dd in range(2)]
            apre = [_mm(ta, rwkv_a2, (mi, dd), a_col=dd) for dd in range(2)]
            gate = _mm(tg, g2p, act="sigmoid")
            pvec = jnp.zeros((16, d), F32)
            for row, val in enumerate((rwkv_k_k[mi], rwkv_k_a[mi], rwkv_r_k[mi].reshape(d), rwkv_lnx_w[mi],
                                       rwkv_lnx_b[mi], rwkv_w0[mi, 0], rwkv_w0[mi, 1], rwkv_a0[mi, 0],
                                       rwkv_a0[mi, 1])):
                pvec = pvec.at[row].set(val)
            y, s_p = _rwkv_scan(r, k, v, wpre, apre, gate, pvec, row_block0=0, n_seq=nb, seq_len=seq,
                                y_buf=jnp.zeros((lay.m, d), BF16))
            y = _rwkv_scan(r, k, v, wpre, apre, gate, pvec, row_block0=lat_block0, n_seq=ndb, seq_len=dseq,
                           s0=state_rwkv, mi=mi, y_buf=y)
            mix = _mm(y, rwkv_w_o, (mi,))
            new_s.append(s_p)
        else:
            hd = _pool_prep(lay_prep, x, g4, mods)
            mix = _mm(hd, pool_w, (mi,), n_groups=len(POOL_WINDOWS), col_scale=pool_scale[mi].reshape(1, d))
        if i % 2 == 0:
            x, h = _resid_norm(lay, x, mix, g4, mods)
            hid = _swiglu_hidden(h, ffn_w_gu, (fi,))
            f = _mm(hid, ffn_w_down, (fi,), tn=_pick(d, (256, 128)), a_single_buffer=True)
        else:
            router = jnp.pad(moe_router[fi], ((0, 0), (0, LANES - n_experts)))
            x, h, route = _resid_norm(lay, x, mix, g4, mods, router=router, n_experts=n_experts)
            tm = _pick(TOP_K * lay.m, (MOE_ROW_TILE, 256, 128, 64, 32, 16, 8))
            row_token, tile_expert, n_used, pos = _moe_plan(route, n_experts, tm)
            xs = _moe_gather(h, row_token, n_used, tm)
            hid = _moe_swiglu(xs, moe_w_gu, fi, tile_expert, n_used, tm)
            ys = _moe_down(hid, moe_w_down, fi, tile_expert, n_used, tm)
            x = _resid_moe(lay, x, ys, pos, route, g4, mods)
            continue
        x = _resid(lay, x, f, g4, mods)

    y_prompt = x[:lay.mp].reshape(nb, seq, d)
    y_sample = x[lay.mp:].reshape(ndb, dseq, d)
    return (y_prompt, y_sample, jnp.stack(new_k, axis=1), jnp.stack(new_v, axis=1), jnp.stack(new_s, axis=1))
```

```python
import functools

import numpy as np
import jax
import jax.numpy as jnp
from jax import lax
from jax.experimental import pallas as pl
from jax.experimental.pallas import tpu as pltpu

F32 = jnp.float32
BF16 = jnp.bfloat16
HIGHEST = lax.Precision.HIGHEST

V7X_VMEM_LIMIT_BYTES = 56 * 1024 * 1024
RWKV_VMEM_BUDGET_BYTES = 46 * 1024 * 1024
LANES = 128

NORM_EPS = 1e-6
NEG_INF = -1e30
ATTN_HEAD_DIM = 128
GRID_W = 64
MAX_WIN_ROWS = 8
WIN_COLS = 16
ATTN_LAT_ROW_GROUP = 4
RWKV_HEAD_DIM = 64
RWKV_CHUNK = 64
LNX_EPS = 64e-5
POOL_WINDOWS = (2, 4, 8, 16)
POOL_HALO = 8
TOP_K = 2
TOK_BLOCK = 256


def _cparams(*sem):
    return pltpu.CompilerParams(dimension_semantics=sem, vmem_limit_bytes=V7X_VMEM_LIMIT_BYTES)


def _pick(n, candidates):
    for c in candidates:
        if c <= n and n % c == 0:
            return c
    return n


def _dot(a, b, precision=None):
    return jnp.dot(a, b, preferred_element_type=F32, precision=precision)


def _dot_nt(a, b, precision=None):
    return lax.dot_general(a, b, (((1,), (1,)), ((), ())), preferred_element_type=F32, precision=precision)


def _rms(x, g):
    return x * lax.rsqrt(jnp.mean(x * x, axis=-1, keepdims=True) + NORM_EPS) * g


def _silu(x):
    return x * jax.nn.sigmoid(x)


class _Layout:
    def __init__(self, n_prompt_seq, prompt_len, n_latent_seq, latent_len, tb):
        assert prompt_len % tb == 0 and latent_len % tb == 0
        self.tb = tb
        self.mp = n_prompt_seq * prompt_len
        self.ms = n_latent_seq * latent_len
        self.m = self.mp + self.ms
        self.prompt_len, self.latent_len = prompt_len, latent_len
        self.n_prompt_seq, self.n_latent_seq = n_prompt_seq, n_latent_seq
        rid, first, last = [], [], []
        for i in range(self.m // tb):
            row = i * tb
            if row < self.mp:
                rid.append(0)
                first.append(int(row % prompt_len == 0))
                last.append(int((row + tb) % prompt_len == 0))
            else:
                rid.append(1 + (row - self.mp) // latent_len)
                first.append(int((row - self.mp) % latent_len == 0))
                last.append(int((row - self.mp + tb) % latent_len == 0))
        self.meta = jnp.asarray(np.array([rid, first, last], np.int32))
        self.nblk = self.m // tb


def _ada_kernel(c_ref, w_ref, b_ref, o_ref):
    s = _silu(c_ref[...]).astype(BF16)
    o_ref[...] = _dot(s, w_ref[...].astype(BF16)) + b_ref[...]


def _ada_all(cond8, ada_w, ada_b):
    depth, d, n = ada_w.shape
    tn = _pick(n, (1024, 512, 256, 128))
    return pl.pallas_call(
        _ada_kernel,
        grid=(depth, n // tn),
        in_specs=[pl.BlockSpec((8, d), lambda l, j: (0, 0)),
                  pl.BlockSpec((None, d, tn), lambda l, j: (l, 0, j)),
                  pl.BlockSpec((None, 1, tn), lambda l, j: (l, 0, j))],
        out_specs=pl.BlockSpec((None, 8, tn), lambda l, j: (l, 0, j)),
        out_shape=jax.ShapeDtypeStruct((depth, 8, n), F32),
        compiler_params=_cparams("parallel", "parallel"),
        name="ada",
    )(cond8, ada_w, ada_b.reshape(depth, 1, n))


def _modulated(x, g, mod_ref, shift_row, scale_row):
    return _rms(x, g) * (1.0 + mod_ref[scale_row:scale_row + 1, :]) + mod_ref[shift_row:shift_row + 1, :]


def _router_route(h, rw_ref, n_experts):
    logits = _dot(h, rw_ref[...], precision=HIGHEST)
    lane = lax.broadcasted_iota(jnp.int32, logits.shape, 1)
    valid = lane < n_experts
    logits = jnp.where(valid, logits, NEG_INF)
    e = jnp.exp(logits - jnp.max(logits, axis=-1, keepdims=True))
    p = e / jnp.sum(e, axis=-1, keepdims=True)
    p = jnp.where(valid, p, -2.0)
    m1 = jnp.max(p, axis=-1, keepdims=True)
    i1 = jnp.min(jnp.where(p == m1, lane, LANES), axis=-1, keepdims=True)
    p2 = jnp.where(lane == i1, -1.0, p)
    m2 = jnp.max(p2, axis=-1, keepdims=True)
    i2 = jnp.min(jnp.where(p2 == m2, lane, LANES), axis=-1, keepdims=True)
    den = m1 + m2
    return (jnp.where(lane == 0, i1.astype(F32), 0.0) + jnp.where(lane == 1, i2.astype(F32), 0.0)
            + jnp.where(lane == 2, m1 / den, 0.0) + jnp.where(lane == 3, m2 / den, 0.0))


def _norm_mod_kernel(meta_ref, x_ref, g_ref, mod_ref, o_ref, *, g_row, shift_row, scale_row):
    h = _modulated(x_ref[...], g_ref[g_row:g_row + 1, :], mod_ref, shift_row, scale_row)
    o_ref[...] = h.astype(o_ref.dtype)


def _norm_mod(lay, x, g4, mods, *, g_row, shift_row, scale_row):
    m, d = x.shape
    tb = lay.tb
    return pl.pallas_call(
        functools.partial(_norm_mod_kernel, g_row=g_row, shift_row=shift_row, scale_row=scale_row),
        grid_spec=pltpu.PrefetchScalarGridSpec(
            num_scalar_prefetch=1, grid=(lay.nblk,),
            in_specs=[pl.BlockSpec((tb, d), lambda i, mt: (i, 0)),
                      pl.BlockSpec((4, d), lambda i, mt: (0, 0)),
                      pl.BlockSpec((None, 6, d), lambda i, mt: (mt[0, i], 0, 0))],
            out_specs=pl.BlockSpec((tb, d), lambda i, mt: (i, 0))),
        out_shape=jax.ShapeDtypeStruct((m, d), BF16),
        compiler_params=_cparams("parallel"),
        name="norm_mod",
    )(lay.meta, x, g4, mods)


def _resid_norm_kernel(meta_ref, x_ref, mix_ref, g_ref, mod_ref, *rest, n_experts):
    if n_experts:
        rw_ref, xo_ref, h_ref, route_ref = rest
    else:
        xo_ref, h_ref = rest
    x = x_ref[...] + mod_ref[2:3, :] * _rms(mix_ref[...], g_ref[1:2, :])
    xo_ref[...] = x
    h = _modulated(x, g_ref[2:3, :], mod_ref, 3, 4)
    h_ref[...] = h.astype(h_ref.dtype)
    if n_experts:
        route_ref[...] = _router_route(h, rw_ref, n_experts)


def _resid_norm(lay, x, mix, g4, mods, router=None, n_experts=0):
    m, d = x.shape
    tb = lay.tb
    row = lambda i, mt: (i, 0)
    in_specs = [pl.BlockSpec((tb, d), row), pl.BlockSpec((tb, d), row),
                pl.BlockSpec((4, d), lambda i, mt: (0, 0)),
                pl.BlockSpec((None, 6, d), lambda i, mt: (mt[0, i], 0, 0))]
    out_specs = [pl.BlockSpec((tb, d), row), pl.BlockSpec((tb, d), row)]
    out_shape = [jax.ShapeDtypeStruct((m, d), F32), jax.ShapeDtypeStruct((m, d), F32 if n_experts else BF16)]
    args = [lay.meta, x, mix, g4, mods]
    if n_experts:
        in_specs.append(pl.BlockSpec((d, LANES), lambda i, mt: (0, 0)))
        out_specs.append(pl.BlockSpec((tb, LANES), row))
        out_shape.append(jax.ShapeDtypeStruct((m, LANES), F32))
        args.append(router)
    return pl.pallas_call(
        functools.partial(_resid_norm_kernel, n_experts=n_experts),
        grid_spec=pltpu.PrefetchScalarGridSpec(
            num_scalar_prefetch=1, grid=(lay.nblk,), in_specs=in_specs, out_specs=out_specs),
        out_shape=out_shape,
        compiler_params=_cparams("parallel"),
        name="resid_norm",
    )(*args)


def _resid_kernel(meta_ref, x_ref, f_ref, g_ref, mod_ref, xo_ref):
    xo_ref[...] = x_ref[...] + mod_ref[5:6, :] * _rms(f_ref[...], g_ref[3:4, :])


def _resid(lay, x, f, g4, mods):
    m, d = x.shape
    tb = lay.tb
    row = lambda i, mt: (i, 0)
    return pl.pallas_call(
        _resid_kernel,
        grid_spec=pltpu.PrefetchScalarGridSpec(
            num_scalar_prefetch=1, grid=(lay.nblk,),
            in_specs=[pl.BlockSpec((tb, d), row), pl.BlockSpec((tb, d), row),
                      pl.BlockSpec((4, d), lambda i, mt: (0, 0)),
                      pl.BlockSpec((None, 6, d), lambda i, mt: (mt[0, i], 0, 0))],
            out_specs=pl.BlockSpec((tb, d), row)),
        out_shape=jax.ShapeDtypeStruct((m, d), F32),
        compiler_params=_cparams("parallel"),
        name="resid",
    )(lay.meta, x, f, g4, mods)


def _mm_kernel(a_ref, w_ref, *rest, act, has_scale):
    if has_scale:
        s_ref, o_ref = rest
    else:
        (o_ref,) = rest
    a = a_ref[...]
    if act == "tanh":
        a = jnp.tanh(a)
    elif act == "sigmoid":
        a = jax.nn.sigmoid(a)
    out = _dot(a.astype(BF16), w_ref[...].astype(BF16))
    if has_scale:
        out = out * s_ref[...]
    o_ref[...] = out.astype(o_ref.dtype)


def _mm(a, w, lead=(), *, out_dtype=F32, act=None, a_col=0, n_groups=1, col_scale=None, tm=None, tn=None,
        a_single_buffer=False):
    m = a.shape[0]
    k, n = w.shape[-2:]
    tm = tm or _pick(m, (1024, 512, 256, 128, 64, 32, 16, 8))
    tn = tn or _pick(n, (512, 256, 128))
    nl = len(lead)
    if n_groups > 1:
        grid = (m // tm, n_groups, n // tn)
        a_spec = pl.BlockSpec((tm, k), lambda i, g, j: (i, g))
        w_spec = pl.BlockSpec((None,) * (nl + 1) + (k, tn), lambda i, g, j: lead + (g, 0, j))
        o_spec = pl.BlockSpec((tm, tn), lambda i, g, j: (i, g * (n // tn) + j))
        s_spec = pl.BlockSpec((1, tn), lambda i, g, j: (0, g * (n // tn) + j))
        sem = ("parallel", "arbitrary", "arbitrary")
    else:
        grid = (m // tm, n // tn)
        a_spec = pl.BlockSpec((tm, k), lambda i, j: (i, a_col),
                              **({"pipeline_mode": pl.Buffered(1)} if a_single_buffer else {}))
        w_spec = pl.BlockSpec((None,) * nl + (k, tn), lambda i, j: lead + (0, j))
        o_spec = pl.BlockSpec((tm, tn), lambda i, j: (i, j))
        s_spec = pl.BlockSpec((1, tn), lambda i, j: (0, j))
        sem = ("parallel", "arbitrary")
    in_specs, args = [a_spec, w_spec], [a, w]
    if col_scale is not None:
        in_specs.append(s_spec)
        args.append(col_scale)
    return pl.pallas_call(
        functools.partial(_mm_kernel, act=act, has_scale=col_scale is not None),
        grid=grid, in_specs=in_specs, out_specs=o_spec,
        out_shape=jax.ShapeDtypeStruct((m, n * n_groups), out_dtype),
        compiler_params=_cparams(*sem),
        name="mm",
    )(*args)


def _swiglu_kernel(a_ref, wg_ref, wu_ref, o_ref):
    a = a_ref[...]
    g = _dot(a, wg_ref[...].astype(BF16))
    u = _dot(a, wu_ref[...].astype(BF16))
    o_ref[...] = (_silu(g) * u).astype(o_ref.dtype)


def _swiglu_hidden(a, w_gu, lead):
    m, d = a.shape
    f = w_gu.shape[-1] // 2
    tm = _pick(m, (1024, 512, 256, 128, 64, 32, 16, 8))
    tn = _pick(f, (256, 128))
    nf = f // tn
    wlead = (None,) * len(lead)
    return pl.pallas_call(
        _swiglu_kernel,
        grid=(m // tm, nf),
        in_specs=[pl.BlockSpec((tm, d), lambda i, j: (i, 0)),
                  pl.BlockSpec(wlead + (d, tn), lambda i, j: lead + (0, j)),
                  pl.BlockSpec(wlead + (d, tn), lambda i, j: lead + (0, nf + j))],
        out_specs=pl.BlockSpec((tm, tn), lambda i, j: (i, j)),
        out_shape=jax.ShapeDtypeStruct((m, f), BF16),
        compiler_params=_cparams("parallel", "arbitrary"),
        name="swiglu",
    )(a, w_gu, w_gu)


MOE_ROW_TILE = 1024
MOE_GATHER_ROWS = 256


def _moe_plan(route, n_experts, tm):
    m = route.shape[0]
    ids = route[:, :TOP_K].astype(jnp.int32).reshape(-1)
    onehot = (ids[:, None] == jnp.arange(n_experts, dtype=jnp.int32)[None, :]).astype(jnp.int32)
    csum = jnp.cumsum(onehot, axis=0)
    counts = csum[-1]
    rank = jnp.sum(onehot * csum, axis=1) - 1
    tiles_per = (counts + tm - 1) // tm
    tile_end = jnp.cumsum(tiles_per)
    row_start = (tile_end - tiles_per) * tm
    pos = jnp.sum(onehot * row_start[None, :], axis=1) + rank
    n_tiles = (TOP_K * m) // tm + n_experts
    row_token = jnp.zeros((n_tiles * tm,), jnp.int32).at[pos].set(jnp.arange(TOP_K * m, dtype=jnp.int32) // TOP_K)
    t = jnp.arange(n_tiles, dtype=jnp.int32)
    n_used = tile_end[-1]
    tile_expert = jnp.sum((t[:, None] >= tile_end[None, :]).astype(jnp.int32), axis=1)
    last_expert = jnp.sum((n_used - 1 >= tile_end).astype(jnp.int32))
    tile_expert = jnp.where(t < n_used, tile_expert, last_expert)
    return row_token, tile_expert, n_used.reshape(1), pos.reshape(m, TOP_K)


def _moe_gather_kernel(nu_ref, tok_ref, tok_next_ref, h_hbm, o_ref, buf, sem, *, tiles_per_row_tile):
    i = pl.program_id(0)
    tg = buf.shape[1]
    n_active = nu_ref[0] * tiles_per_row_tile
    slot = i % 2

    def copy(idx_ref, k, s):
        return pltpu.make_async_copy(h_hbm.at[pl.ds(idx_ref[0, k], 1), :], buf.at[s, pl.ds(k, 1), :], sem.at[s])

    def start_tile(idx_ref, s):
        def start(k, carry):
            copy(idx_ref, k, s).start()
            return carry

        lax.fori_loop(0, tg, start, 0)

    @pl.when(i == 0)
    def _():
        start_tile(tok_ref, 0)

    @pl.when(i + 1 < n_active)
    def _():
        start_tile(tok_next_ref, 1 - slot)

    @pl.when(i < n_active)
    def _():
        def wait(k, carry):
            copy(tok_ref, k, slot).wait()
            return carry

        lax.fori_loop(0, tg, wait, 0)
        o_ref[...] = buf[slot].astype(o_ref.dtype)

    @pl.when(i >= n_active)
    def _():
        o_ref[...] = jnp.zeros_like(o_ref)


def _moe_gather(h, row_token, n_used, tm):
    m, d = h.shape
    rows = row_token.shape[0]
    tg = _pick(tm, (MOE_GATHER_ROWS, 128, 64, 32, 16, 8))
    nt = rows // tg
    tok = row_token.reshape(nt, 1, tg)
    return pl.pallas_call(
        functools.partial(_moe_gather_kernel, tiles_per_row_tile=tm // tg),
        grid_spec=pltpu.PrefetchScalarGridSpec(
            num_scalar_prefetch=1, grid=(nt,),
            in_specs=[pl.BlockSpec((None, 1, tg), lambda i, nu: (i, 0, 0), memory_space=pltpu.SMEM),
                      pl.BlockSpec((None, 1, tg), lambda i, nu: (jnp.minimum(i + 1, nt - 1), 0, 0),
                                   memory_space=pltpu.SMEM),
                      pl.BlockSpec(memory_space=pl.ANY)],
            out_specs=pl.BlockSpec((tg, d), lambda i, nu: (i, 0)),
            scratch_shapes=[pltpu.VMEM((2, tg, d), F32), pltpu.SemaphoreType.DMA((2,))]),
        out_shape=jax.ShapeDtypeStruct((rows, d), BF16),
        compiler_params=_cparams("arbitrary"),
        name="moe_gather",
    )(n_used, tok, tok, h)


def _moe_swiglu_kernel(te_ref, nu_ref, a_ref, wg_ref, wu_ref, o_ref):
    @pl.when(pl.program_id(0) < nu_ref[0])
    def _():
        a = a_ref[...]
        g = _dot(a, wg_ref[...].astype(BF16))
        u = _dot(a, wu_ref[...].astype(BF16))
        o_ref[...] = (_silu(g) * u).astype(o_ref.dtype)

    @pl.when(pl.program_id(0) >= nu_ref[0])
    def _():
        o_ref[...] = jnp.zeros_like(o_ref)


def _moe_swiglu(a, w_gu, fi, tile_expert, n_used, tm):
    rows, d = a.shape
    f = w_gu.shape[-1] // 2
    tn = _pick(f, (256, 128))
    nf = f // tn
    return pl.pallas_call(
        _moe_swiglu_kernel,
        grid_spec=pltpu.PrefetchScalarGridSpec(
            num_scalar_prefetch=2, grid=(rows // tm, nf),
            in_specs=[pl.BlockSpec((tm, d), lambda t, j, te, nu: (t, 0)),
                      pl.BlockSpec((None, None, d, tn), lambda t, j, te, nu: (fi, te[t], 0, j)),
                      pl.BlockSpec((None, None, d, tn), lambda t, j, te, nu: (fi, te[t], 0, nf + j))],
            out_specs=pl.BlockSpec((tm, tn), lambda t, j, te, nu: (t, j))),
        out_shape=jax.ShapeDtypeStruct((rows, f), BF16),
        compiler_params=_cparams("parallel", "arbitrary"),
        name="moe_swiglu",
    )(tile_expert, n_used, a, w_gu, w_gu)


def _moe_down_kernel(te_ref, nu_ref, a_ref, w_ref, o_ref):
    @pl.when(pl.program_id(0) < nu_ref[0])
    def _():
        o_ref[...] = _dot(a_ref[...], w_ref[...].astype(BF16))

    @pl.when(pl.program_id(0) >= nu_ref[0])
    def _():
        o_ref[...] = jnp.zeros_like(o_ref)


def _moe_down(a, w_down, fi, tile_expert, n_used, tm):
    rows, f = a.shape
    d = w_down.shape[-1]
    tn = _pick(d, (512, 256, 128))
    return pl.pallas_call(
        _moe_down_kernel,
        grid_spec=pltpu.PrefetchScalarGridSpec(
            num_scalar_prefetch=2, grid=(rows // tm, d // tn),
            in_specs=[pl.BlockSpec((tm, f), lambda t, j, te, nu: (t, 0)),
                      pl.BlockSpec((None, None, f, tn), lambda t, j, te, nu: (fi, te[t], 0, j))],
            out_specs=pl.BlockSpec((tm, tn), lambda t, j, te, nu: (t, j))),
        out_shape=jax.ShapeDtypeStruct((rows, d), F32),
        compiler_params=_cparams("parallel", "arbitrary"),
        name="moe_down",
    )(tile_expert, n_used, a, w_down)


def _resid_moe_kernel(meta_ref, pos_ref, pos_next_ref, x_ref, route_ref, g_ref, mod_ref, y_hbm, xo_ref, buf, sem):
    i = pl.program_id(0)
    tb = x_ref.shape[0]
    slot = i % 2

    def copy(idx_ref, k, e, s):
        return pltpu.make_async_copy(y_hbm.at[pl.ds(idx_ref[0, TOP_K * k + e], 1), :],
                                     buf.at[s, e, pl.ds(k, 1), :], sem.at[s])

    def start_tile(idx_ref, s):
        def start(k, carry):
            for e in range(TOP_K):
                copy(idx_ref, k, e, s).start()
            return carry

        lax.fori_loop(0, tb, start, 0)

    @pl.when(i == 0)
    def _():
        start_tile(pos_ref, 0)

    @pl.when(i + 1 < pl.num_programs(0))
    def _():
        start_tile(pos_next_ref, 1 - slot)

    def wait(k, carry):
        for e in range(TOP_K):
            copy(pos_ref, k, e, slot).wait()
        return carry

    lax.fori_loop(0, tb, wait, 0)
    route = route_ref[...]
    f = route[:, 2:3] * buf[slot, 0] + route[:, 3:4] * buf[slot, 1]
    xo_ref[...] = x_ref[...] + mod_ref[5:6, :] * _rms(f, g_ref[3:4, :])


def _resid_moe(lay, x, y_sorted, pos, route, g4, mods):
    m, d = x.shape
    tb = lay.tb
    row = lambda i, mt: (i, 0)
    pos3 = pos.reshape(lay.nblk, 1, TOP_K * tb)
    return pl.pallas_call(
        _resid_moe_kernel,
        grid_spec=pltpu.PrefetchScalarGridSpec(
            num_scalar_prefetch=1, grid=(lay.nblk,),
            in_specs=[pl.BlockSpec((None, 1, TOP_K * tb), lambda i, mt: (i, 0, 0), memory_space=pltpu.SMEM),
                      pl.BlockSpec((None, 1, TOP_K * tb), lambda i, mt: (jnp.minimum(i + 1, lay.nblk - 1), 0, 0),
                                   memory_space=pltpu.SMEM),
                      pl.BlockSpec((tb, d), row), pl.BlockSpec((tb, LANES), row),
                      pl.BlockSpec((4, d), lambda i, mt: (0, 0)),
                      pl.BlockSpec((None, 6, d), lambda i, mt: (mt[0, i], 0, 0)),
                      pl.BlockSpec(memory_space=pl.ANY)],
            out_specs=pl.BlockSpec((tb, d), row),
            scratch_shapes=[pltpu.VMEM((2, TOP_K, tb, d), F32), pltpu.SemaphoreType.DMA((2,))]),
        out_shape=jax.ShapeDtypeStruct((m, d), F32),
        compiler_params=_cparams("arbitrary"),
        name="resid_moe",
    )(lay.meta, pos3, pos3, x, route, g4, mods, y_sorted)


def _softmax_rows(s):
    e = jnp.exp(s - jnp.max(s, axis=-1, keepdims=True))
    return e * (1.0 / jnp.sum(e, axis=-1, keepdims=True))


def _attn_ctx_kernel(q_ref, k_ref, v_ref, _, o_ref, *, heads, scale):
    for h in range(heads):
        sl = slice(h * ATTN_HEAD_DIM, (h + 1) * ATTN_HEAD_DIM)
        s = _dot_nt(q_ref[:, sl].astype(BF16), k_ref[:, sl].astype(BF16)) * scale
        p = _softmax_rows(s).astype(BF16)
        o_ref[:, sl] = _dot(p, v_ref[:, sl].astype(BF16)).astype(o_ref.dtype)


def _attn_context(qkv, n_seq, seq_len, d, o_buf):
    n_heads = d // ATTN_HEAD_DIM
    hb = _pick(n_heads, (4, 2, 1))
    wb = hb * ATTN_HEAD_DIM
    ncb = d // wb
    return pl.pallas_call(
        functools.partial(_attn_ctx_kernel, heads=hb, scale=ATTN_HEAD_DIM ** -0.5),
        grid=(n_seq, ncb),
        in_specs=[pl.BlockSpec((seq_len, wb), lambda b, h: (b, h)),
                  pl.BlockSpec((seq_len, wb), lambda b, h: (b, ncb + h)),
                  pl.BlockSpec((seq_len, wb), lambda b, h: (b, 2 * ncb + h)),
                  pl.BlockSpec(memory_space=pl.ANY)],
        out_specs=pl.BlockSpec((seq_len, wb), lambda b, h: (b, h)),
        out_shape=jax.ShapeDtypeStruct(o_buf.shape, o_buf.dtype),
        input_output_aliases={3: 0},
        compiler_params=_cparams("parallel", "parallel"),
        name="attn_context",
    )(qkv, qkv, qkv, o_buf)


def _attn_lat_kernel(q_ref, k_ref, v_ref, ck_ref, cv_ref, tb_ref, _, o_ref, *, rows, wr, scale):
    kb = k_ref[...].astype(BF16)
    vb = v_ref[...].astype(BF16)
    ckb = ck_ref[...].astype(BF16)
    cvb = cv_ref[...].astype(BF16)
    group = ATTN_LAT_ROW_GROUP if rows % ATTN_LAT_ROW_GROUP == 0 else 1
    row_start = [min(max(r - wr // 2, 0), rows - wr) for r in range(rows)]
    neg = jnp.full((GRID_W, GRID_W), NEG_INF, F32)
    biases = {}
    for r0 in range(0, rows, group):
        qrows = range(r0, r0 + group)
        k0 = row_start[r0]
        k1 = row_start[r0 + group - 1] + wr
        pattern = tuple(tuple(kr - r + MAX_WIN_ROWS - 1 if row_start[r] <= kr < row_start[r] + wr else None
                              for kr in range(k0, k1)) for r in qrows)
        if pattern not in biases:
            biases[pattern] = jnp.concatenate(
                [jnp.concatenate([neg if dr is None else tb_ref[dr] for dr in prow], axis=1) for prow in pattern],
                axis=0)
        q = q_ref[r0 * GRID_W:(r0 + group) * GRID_W, :].astype(BF16)
        kl = kb[k0 * GRID_W:k1 * GRID_W]
        vl = vb[k0 * GRID_W:k1 * GRID_W]
        s_loc = _dot_nt(q, kl) * scale + biases[pattern]
        s_ctx = _dot_nt(q, ckb) * scale
        mx = jnp.maximum(jnp.max(s_loc, axis=-1, keepdims=True), jnp.max(s_ctx, axis=-1, keepdims=True))
        e_loc = jnp.exp(s_loc - mx)
        e_ctx = jnp.exp(s_ctx - mx)
        inv = 1.0 / (jnp.sum(e_loc, axis=-1, keepdims=True) + jnp.sum(e_ctx, axis=-1, keepdims=True))
        o = (_dot(e_loc.astype(BF16), vl) + _dot(e_ctx.astype(BF16), cvb)) * inv
        o_ref[r0 * GRID_W:(r0 + group) * GRID_W, :] = o.astype(o_ref.dtype)


def _rel_bias_table(rpb):
    qc = np.arange(GRID_W)[:, None]
    kc = np.arange(GRID_W)[None, :]
    ws = np.clip(qc - WIN_COLS // 2, 0, GRID_W - WIN_COLS)
    in_win = (kc >= ws) & (kc < ws + WIN_COLS)
    dcol = np.clip(kc - qc, 1 - WIN_COLS, WIN_COLS - 1) + WIN_COLS - 1
    return jnp.where(in_win[None, None], rpb.astype(F32)[:, :, dcol], NEG_INF)


def _attn_latent(qkv, cache_k, cache_v, mi, rpb, row_block0, n_seq, seq_len, d, o_buf):
    n_heads = d // ATTN_HEAD_DIM
    rows = seq_len // GRID_W
    wr = min(MAX_WIN_ROWS, rows)
    past = cache_k.shape[2]
    ck = cache_k.reshape(cache_k.shape[0], cache_k.shape[1], past, d)
    cv = cache_v.reshape(cache_v.shape[0], cache_v.shape[1], past, d)
    table = _rel_bias_table(rpb)
    hd = ATTN_HEAD_DIM
    return pl.pallas_call(
        functools.partial(_attn_lat_kernel, rows=rows, wr=wr, scale=ATTN_HEAD_DIM ** -0.5),
        grid=(n_seq, n_heads),
        in_specs=[pl.BlockSpec((seq_len, hd), lambda b, h: (row_block0 + b, h)),
                  pl.BlockSpec((seq_len, hd), lambda b, h: (row_block0 + b, n_heads + h)),
                  pl.BlockSpec((seq_len, hd), lambda b, h: (row_block0 + b, 2 * n_heads + h)),
                  pl.BlockSpec((None, None, past, hd), lambda b, h: (b, mi, 0, h)),
                  pl.BlockSpec((None, None, past, hd), lambda b, h: (b, mi, 0, h)),
                  pl.BlockSpec((None, 2 * MAX_WIN_ROWS - 1, GRID_W, GRID_W), lambda b, h: (h, 0, 0, 0)),
                  pl.BlockSpec(memory_space=pl.ANY)],
        out_specs=pl.BlockSpec((seq_len, hd), lambda b, h: (row_block0 + b, h)),
        out_shape=jax.ShapeDtypeStruct(o_buf.shape, o_buf.dtype),
        input_output_aliases={6: 0},
        compiler_params=_cparams("parallel", "parallel"),
        name="attn_latent",
    )(qkv, qkv, qkv, ck, cv, table, o_buf)


def _halo_rows(meta_ref, h_prev_blk, h_next_blk):
    i = pl.program_id(0)
    hp = jnp.where(meta_ref[1, i] == 1, 0.0, h_prev_blk[POOL_HALO - 1:POOL_HALO, :])
    hn = jnp.where(meta_ref[2, i] == 1, 0.0, h_next_blk[0:1, :])
    return hp, hn


def _rwkv_prep_kernel(meta_ref, x_ref, xp_ref, xn_ref, g_ref, mod_ref, mu_ref, *o_refs):
    g = g_ref[0:1, :]
    h = _modulated(x_ref[...], g, mod_ref, 0, 1)
    hp, hn = _halo_rows(meta_ref, _modulated(xp_ref[...], g, mod_ref, 0, 1),
                        _modulated(xn_ref[...], g, mod_ref, 0, 1))
    tb = h.shape[0]
    row = lax.broadcasted_iota(jnp.int32, h.shape, 0)
    prev = jnp.where(row == 0, hp, pltpu.roll(h, 1, 0))
    nxt = jnp.where(row == tb - 1, hn, pltpu.roll(h, tb - 1, 0))
    xx = 0.5 * (prev + nxt) - h
    for n, o_ref in enumerate(o_refs):
        o_ref[...] = (h + xx * mu_ref[n:n + 1, :]).astype(o_ref.dtype)


def _halo_specs(lay, d):
    tb = lay.tb
    per = tb // POOL_HALO
    last = lay.nblk * per - 1
    return [pl.BlockSpec((tb, d), lambda i, mt: (i, 0)),
            pl.BlockSpec((POOL_HALO, d), lambda i, mt: (jnp.maximum(i * per - 1, 0), 0)),
            pl.BlockSpec((POOL_HALO, d), lambda i, mt: (jnp.minimum((i + 1) * per, last), 0))]


def _rwkv_prep(lay, x, g4, mods, mu):
    m, d = x.shape
    tb = lay.tb
    return pl.pallas_call(
        _rwkv_prep_kernel,
        grid_spec=pltpu.PrefetchScalarGridSpec(
            num_scalar_prefetch=1, grid=(lay.nblk,),
            in_specs=_halo_specs(lay, d) + [
                pl.BlockSpec((4, d), lambda i, mt: (0, 0)),
                pl.BlockSpec((None, 6, d), lambda i, mt: (mt[0, i], 0, 0)),
                pl.BlockSpec((6, d), lambda i, mt: (0, 0))],
            out_specs=[pl.BlockSpec((tb, d), lambda i, mt: (i, 0))] * 6),
        out_shape=[jax.ShapeDtypeStruct((m, d), BF16)] * 6,
        compiler_params=_cparams("parallel"),
        name="rwkv_prep",
    )(lay.meta, x, x, x, g4, mods, mu)


def _split(x):
    hi = x.astype(BF16)
    return hi, (x - hi.astype(F32)).astype(BF16)


def _dot3(a, b):
    (ah, al), (bh, bl) = a, b
    return _dot(jnp.concatenate([ah, ah, al], axis=1), jnp.concatenate([bh, bl, bh], axis=0))


def _dot3_nt(a, b):
    (ah, al), (bh, bl) = a, b
    return _dot_nt(jnp.concatenate([ah, ah, al], axis=1), jnp.concatenate([bh, bl, bh], axis=1))


def _rwkv_scan_kernel(*refs, seq_len, has_s0, pairs):
    if has_s0:
        (r_ref, k_ref, v_ref, w0_ref, w1_ref, a0_ref, a1_ref, g_ref, pv_ref, s0_ref, _, y_ref, yf_ref, yb_ref) = refs
        sf_ref = None
    else:
        (r_ref, k_ref, v_ref, w0_ref, w1_ref, a0_ref, a1_ref, g_ref, pv_ref, _, y_ref, sf_ref, yf_ref, yb_ref) = refs
        s0_ref = None
    c = RWKV_CHUNK
    n = RWKV_HEAD_DIM
    nchunks = seq_len // c
    wpre_refs = (w0_ref, w1_ref)
    apre_refs = (a0_ref, a1_ref)
    def iota(shape, dim):
        return lax.broadcasted_iota(jnp.int32, shape, dim)

    head0 = iota((c, LANES), 1) < n
    seg = (iota((LANES, LANES), 0) // n == iota((LANES, LANES), 1) // n)
    seg_f = seg.astype(F32)
    seg_b = seg_f.astype(BF16)
    tt = iota((c, 2 * c), 0)
    ss = iota((c, 2 * c), 1) % c
    lo_half = iota((c, 2 * c), 1) < c
    strict = (ss < tt, ss > tt)
    incl = (ss <= tt, ss >= tt)
    t2 = iota((c, c), 0)
    s2 = iota((c, c), 1)
    tri = ((s2 <= t2).astype(F32).astype(BF16), (s2 >= t2).astype(F32).astype(BF16))

    def seg_sum(x):
        hi, lo = _split(x)
        return _dot(jnp.concatenate([hi, lo], axis=1), jnp.concatenate([seg_b, seg_b], axis=0))

    head0_b = head0.astype(F32).astype(BF16)
    head1_b = (1.0 - head0.astype(F32)).astype(BF16)
    lo_half_b = lo_half.astype(F32).astype(BF16)
    hi_half_b = (1.0 - lo_half.astype(F32)).astype(BF16)

    def stack2(x):
        return jnp.concatenate([x * head0_b, x * head1_b], axis=0)

    def stack2_parts(parts):
        return tuple(stack2(p) for p in parts)

    def blockdiag(x):
        return jnp.concatenate([x * lo_half_b, x * hi_half_b], axis=0)

    def cat0(a, b):
        return tuple(jnp.concatenate([x, y], axis=0) for x, y in zip(a, b))

    def features(rows, lanes, d):
        k = k_ref[rows, lanes]
        kk = k * pv_ref[0:1, lanes]
        kk = kk / jnp.maximum(jnp.sqrt(seg_sum(kk * kk)), 1e-12)
        wx = -(pv_ref[5 + d:6 + d, lanes] + wpre_refs[d][rows, lanes])
        w_log = -(jnp.maximum(wx, 0.0) + jnp.log(1.0 + jnp.exp(-jnp.abs(wx)))) - 0.5
        logw = -jnp.exp(w_log)
        a = jax.nn.sigmoid(pv_ref[7 + d:8 + d, lanes] + apre_refs[d][rows, lanes])
        kd = k * (1.0 + (a - 1.0) * pv_ref[1:2, lanes])
        return kk, kd, kk * a, logw

    def chunk(rows, lanes, d, s_bd):
        r = r_ref[rows, lanes]
        v = v_ref[rows, lanes]
        kk, kd, bb, logw = features(rows, lanes, d)
        yield
        l1 = logw.astype(BF16)
        l2 = (logw - l1.astype(F32)).astype(BF16)
        l3 = (logw - l1.astype(F32) - l2.astype(F32)).astype(BF16)
        cum = _dot(jnp.concatenate([tri[d]] * 3, axis=1), jnp.concatenate([l1, l2, l3], axis=0))
        yield
        total = cum[c - 1:c, :] if d == 0 else cum[0:1, :]
        e_inv = jnp.exp(-cum)
        e_rest = jnp.exp(total - cum)
        at = _split(kk * jnp.exp(cum - logw))
        rt = (r * jnp.exp(cum)).astype(BF16)
        kb = cat0(stack2_parts(_split(kd * e_inv)), stack2_parts(_split(bb * e_inv)))
        s_parts = _split(s_bd)
        kbs = cat0(kb, s_parts)
        sa = _dot3_nt(at, kbs)
        sr = _dot_nt(rt, kbs[0])
        yield
        zero = jnp.zeros((c, 2 * c), F32)
        m_cat = jnp.where(strict[d], sa[:, :2 * c], zero)
        l_cat = jnp.where(strict[d], sa[:, 2 * c:4 * c], zero)
        rk_cat = jnp.where(incl[d], sr[:, :2 * c], zero)
        rb_cat = jnp.where(incl[d], sr[:, 2 * c:4 * c], zero)
        y0 = sr[:, 4 * c:]
        v_st = stack2_parts(_split(v))
        x = sa[:, 4 * c:] + _dot3(_split(m_cat), v_st)
        yield
        p = -l_cat
        n_stage = c.bit_length() - 1
        for stage in range(n_stage):
            pp = _split(p)
            xs = stack2_parts(_split(x))
            if stage + 1 < n_stage:
                px = _dot3(pp, tuple(jnp.concatenate([xi, blockdiag(pi)], axis=1) for xi, pi in zip(xs, pp)))
                x = x + px[:, :LANES]
                p = px[:, LANES:]
            else:
                x = x + _dot3(pp, xs)
            yield
        u = x
        u_parts = _split(u)
        y = y0 + _dot(
            jnp.concatenate([rk_cat, rb_cat], axis=1).astype(BF16),
            jnp.concatenate([v_st[0], -stack2(u_parts[0])], axis=0))
        vu_t = _split(jnp.concatenate([v, -u], axis=0).T)
        kb_rest = cat0(_split(kd * e_rest), _split(bb * e_rest))
        s_new = s_bd * jnp.exp(total) + seg_f * _dot3(vu_t, kb_rest)
        return y, s_new

    def run_lockstep(gens):
        results = [None] * len(gens)
        active = list(range(len(gens)))
        while active:
            for i in list(active):
                try:
                    next(gens[i])
                except StopIteration as stop:
                    results[i] = stop.value
                    active.remove(i)
        return results

    chains = [(pi, d) for pi in range(pairs) for d in range(2)]
    if has_s0:
        z = jnp.zeros((n, n), F32)
        s_init = tuple(
            jnp.concatenate([jnp.concatenate([s0_ref[d, 2 * pi], z], axis=1),
                             jnp.concatenate([z, s0_ref[d, 2 * pi + 1]], axis=1)], axis=0) for pi, d in chains)
    else:
        s_init = tuple(jnp.zeros((LANES, LANES), F32) for _ in chains)

    def body(ci, carry):
        rows_fb = (pl.ds(pl.multiple_of(ci * c, c), c), pl.ds(pl.multiple_of((nchunks - 1 - ci) * c, c), c))
        lanes = [slice(pi * LANES, (pi + 1) * LANES) for pi, _ in chains]
        results = run_lockstep([chunk(rows_fb[d], ln, d, s_bd)
                                for (_, d), ln, s_bd in zip(chains, lanes, carry)])
        for (_, d), ln, (y, _) in zip(chains, lanes, results):
            (yf_ref, yb_ref)[d][rows_fb[d], ln] = y
        return tuple(s_bd for _, s_bd in results)

    s_fin = lax.fori_loop(0, nchunks, body, s_init)
    if sf_ref is not None:
        for (pi, d), s_bd in zip(chains, s_fin):
            sf_ref[d, 2 * pi] = s_bd[:n, :n]
            sf_ref[d, 2 * pi + 1] = s_bd[n:, n:]

    fr = 2 * c if seq_len % (2 * c) == 0 else c

    def finish_rows(rows, lanes):
        y = yf_ref[rows, lanes] + yb_ref[rows, lanes]
        a_sum = sum(jax.nn.sigmoid(pv_ref[7 + d:8 + d, lanes] + apre_refs[d][rows, lanes]) for d in range(2))
        kd_sum = k_ref[rows, lanes] * (2.0 + (a_sum - 2.0) * pv_ref[1:2, lanes])
        sums = seg_sum(jnp.concatenate([y, r_ref[rows, lanes] * kd_sum * pv_ref[2:3, lanes]], axis=0))
        yield
        yc = y - sums[:fr] * (1.0 / n)
        var = seg_sum(yc * yc) * (1.0 / n)
        yield
        y = yc * lax.rsqrt(var + LNX_EPS) * pv_ref[3:4, lanes] + pv_ref[4:5, lanes] + sums[fr:] * v_ref[rows, lanes]
        y_ref[rows, lanes] = (y * g_ref[rows, lanes]).astype(y_ref.dtype)

    def finish(ci, carry):
        rows = pl.ds(pl.multiple_of(ci * fr, fr), fr)
        run_lockstep([finish_rows(rows, slice(pi * LANES, (pi + 1) * LANES)) for pi in range(pairs)])
        return carry

    lax.fori_loop(0, seq_len // fr, finish, 0)


def _rwkv_scan(r, k, v, wpre, apre, g, pvec, *, row_block0, n_seq, seq_len, s0=None, mi=0, y_buf=None):
    m, d = r.shape
    n = RWKV_HEAD_DIM
    def fits(p, bufs):
        return (d // LANES) % p == 0 and (bufs * 8 + 2 + 1) * seq_len * p * LANES * 4 <= RWKV_VMEM_BUDGET_BYTES

    pairs, bufs = next((p, b) for p in (4, 2, 1) for b in (2, 1) if fits(p, b))
    wb = pairs * LANES
    nblk = d // wb
    seq_map = lambda b, p: (row_block0 + b, p)
    blk = pl.BlockSpec((seq_len, wb), seq_map)
    blk_in = blk if bufs == 2 else pl.BlockSpec((seq_len, wb), seq_map, pipeline_mode=pl.Buffered(1))
    in_specs = [blk_in] * 8 + [pl.BlockSpec((16, wb), lambda b, p: (0, p))]
    args = [r, k, v, wpre[0], wpre[1], apre[0], apre[1], g, pvec]
    y_shape = jax.ShapeDtypeStruct((m, d), BF16)
    if s0 is not None:
        in_specs.append(pl.BlockSpec((None, None, 2, 2 * pairs, n, n), lambda b, p: (b, mi, 0, p, 0, 0)))
        args.append(s0)
    in_specs.append(pl.BlockSpec(memory_space=pl.ANY))
    args.append(y_buf)
    aliases = {len(args) - 1: 0}
    if s0 is not None:
        out_specs, out_shape = blk, y_shape
    else:
        out_specs = [blk, pl.BlockSpec((None, 2, 2 * pairs, n, n), lambda b, p: (b, 0, p, 0, 0))]
        out_shape = [y_shape, jax.ShapeDtypeStruct((n_seq, 2, d // n, n, n), F32)]
    return pl.pallas_call(
        functools.partial(_rwkv_scan_kernel, seq_len=seq_len, has_s0=s0 is not None, pairs=pairs),
        grid=(n_seq, nblk),
        in_specs=in_specs, out_specs=out_specs, out_shape=out_shape, input_output_aliases=aliases,
        scratch_shapes=[pltpu.VMEM((seq_len, wb), F32), pltpu.VMEM((seq_len, wb), F32)],
        compiler_params=_cparams("parallel", "parallel"),
        name="rwkv_scan",
    )(*args)


def _pool_prep_kernel(meta_ref, x_ref, xp_ref, xn_ref, g_ref, mod_ref, o_ref):
    i = pl.program_id(0)
    g = g_ref[0:1, :]
    h = _modulated(x_ref[...], g, mod_ref, 0, 1)
    hp = jnp.where(meta_ref[1, i] == 1, 0.0, _modulated(xp_ref[...], g, mod_ref, 0, 1))
    hn = jnp.where(meta_ref[2, i] == 1, 0.0, _modulated(xn_ref[...], g, mod_ref, 0, 1))
    tb, d = h.shape
    pd = d // len(POOL_WINDOWS)
    ext = jnp.concatenate([hp, h, hn], axis=0)
    ne = tb + 2 * POOL_HALO
    row = lax.broadcasted_iota(jnp.int32, (tb, pd), 0)
    at_first = (meta_ref[1, i] == 1).astype(jnp.int32)
    at_last = (meta_ref[2, i] == 1).astype(jnp.int32)
    acc = ext[:, 0:d] + pltpu.roll(ext, 1, 0)
    half = 1
    for gi, win in enumerate(POOL_WINDOWS):
        if gi > 0:
            sub = acc[:, pd:]
            acc = pltpu.roll(sub, half, 0) + pltpu.roll(sub, ne - half, 0)
            half *= 2
        wsum = acc[POOL_HALO:POOL_HALO + tb, 0:pd]
        missing = (at_first * jnp.maximum(win // 2 - row, 0) + at_last * jnp.maximum(row + win // 2 - tb, 0))
        cnt = (win - missing).astype(F32)
        o_ref[:, gi * pd:(gi + 1) * pd] = (wsum / cnt - h[:, gi * pd:(gi + 1) * pd]).astype(o_ref.dtype)


def _pool_prep(lay, x, g4, mods):
    m, d = x.shape
    tb = lay.tb
    return pl.pallas_call(
        _pool_prep_kernel,
        grid_spec=pltpu.PrefetchScalarGridSpec(
            num_scalar_prefetch=1, grid=(lay.nblk,),
            in_specs=_halo_specs(lay, d) + [
                pl.BlockSpec((4, d), lambda i, mt: (0, 0)),
                pl.BlockSpec((None, 6, d), lambda i, mt: (mt[0, i], 0, 0))],
            out_specs=pl.BlockSpec((tb, d), lambda i, mt: (i, 0))),
        out_shape=jax.ShapeDtypeStruct((m, d), BF16),
        compiler_params=_cparams("parallel"),
        name="pool_prep",
    )(lay.meta, x, x, x, g4, mods)


def kernel(x_prompt, x_sample, cache_k, cache_v, state_rwkv, c, c_ctx, ada_w, ada_b, norm_g,
           attn_w_qkv, attn_w_o, attn_rpb, rwkv_mu, rwkv_w_r, rwkv_w_k, rwkv_w_v, rwkv_w_o,
           rwkv_w0, rwkv_w1, rwkv_w2, rwkv_a0, rwkv_a1, rwkv_a2, rwkv_g1, rwkv_g2, rwkv_k_k, rwkv_k_a,
           rwkv_r_k, rwkv_lnx_w, rwkv_lnx_b, pool_w, pool_scale, ffn_w_gu, ffn_w_down,
           moe_router, moe_w_gu, moe_w_down):
    nb, seq, d = x_prompt.shape
    ndb, dseq, _ = x_sample.shape
    depth = ada_w.shape[0]
    n_experts = moe_router.shape[-1]
    lay = _Layout(nb, seq, ndb, dseq, _pick(np.gcd(seq, dseq), (TOK_BLOCK, 128, 64, 32, 16)))
    lay_prep = _Layout(nb, seq, ndb, dseq, _pick(np.gcd(seq, dseq), (TOK_BLOCK // 2, 64, 32, 16)))
    assert lay.mp % dseq == 0
    lat_block0 = lay.mp // dseq

    x = jnp.concatenate([x_prompt.reshape(lay.mp, d), x_sample.reshape(lay.ms, d)], axis=0)
    cond8 = jnp.zeros((8, d), F32).at[0].set(c_ctx).at[1:1 + ndb].set(c)
    mods_all = _ada_all(cond8, ada_w, ada_b).reshape(depth, 8, 6, d)

    new_k, new_v, new_s = [], [], []
    for i in range(depth):
        kind, mi, fi = i % 3, i // 3, i // 2
        mods = mods_all[i]
        g4 = norm_g[i]
        if kind == 0:
            h = _norm_mod(lay, x, g4, mods, g_row=0, shift_row=0, scale_row=1)
            qkv = _mm(h, attn_w_qkv, (mi,))
            o = _attn_context(qkv, nb, seq, d, jnp.zeros((lay.m, d), BF16))
            o = _attn_latent(qkv, cache_k, cache_v, mi, attn_rpb[mi], lat_block0, ndb, dseq, d, o)
            mix = _mm(o, attn_w_o, (mi,))
            heads = d // ATTN_HEAD_DIM
            new_k.append(qkv[:lay.mp, d:2 * d].reshape(nb, seq, heads, ATTN_HEAD_DIM))
            new_v.append(qkv[:lay.mp, 2 * d:].reshape(nb, seq, heads, ATTN_HEAD_DIM))
        elif kind == 1:
            xr, xw, xk, xv, xa, xg = _rwkv_prep(lay_prep, x, g4, mods, rwkv_mu[mi])
            r = _mm(xr, rwkv_w_r, (mi,))
            k = _mm(xk, rwkv_w_k, (mi,))
            v = _mm(xv, rwkv_w_v, (mi,))
            lw, la = rwkv_w1.shape[-1], rwkv_a1.shape[-1]
            w1cat = jnp.moveaxis(rwkv_w1[mi], 0, 1).reshape(d, 2 * lw)
            a1cat = jnp.moveaxis(rwkv_a1[mi], 0, 1).reshape(d, 2 * la)
            lg = rwkv_g1.shape[-1]
            lgp = -(-lg // LANES) * LANES
            g1p = jnp.pad(rwkv_g1[mi], ((0, 0), (0, lgp - lg)))
            g2p = jnp.pad(rwkv_g2[mi], ((0, lgp - lg), (0, 0)))
            tw = _mm(xw, w1cat)
            ta = _mm(xa, a1cat)
            tg = _mm(xg, g1p)
            wpre = [_mm(tw, rwkv_w2, (mi, dd), act="tanh", a_col=dd) for dd in range(2)]
            apre = [_mm(ta, rwkv_a2, (mi, dd), a_col=dd) for dd in range(2)]
            gate = _mm(tg, g2p, act="sigmoid")
            pvec = jnp.zeros((16, d), F32)
            for row, val in enumerate((rwkv_k_k[mi], rwkv_k_a[mi], rwkv_r_k[mi].reshape(d), rwkv_lnx_w[mi],
                                       rwkv_lnx_b[mi], rwkv_w0[mi, 0], rwkv_w0[mi, 1], rwkv_a0[mi, 0],
                                       rwkv_a0[mi, 1])):
                pvec = pvec.at[row].set(val)
            y, s_p = _rwkv_scan(r, k, v, wpre, apre, gate, pvec, row_block0=0, n_seq=nb, seq_len=seq,
                                y_buf=jnp.zeros((lay.m, d), BF16))
            y = _rwkv_scan(r, k, v, wpre, apre, gate, pvec, row_block0=lat_block0, n_seq=ndb, seq_len=dseq,
                           s0=state_rwkv, mi=mi, y_buf=y)
            mix = _mm(y, rwkv_w_o, (mi,))
            new_s.append(s_p)
        else:
            hd = _pool_prep(lay_prep, x, g4, mods)
            mix = _mm(hd, pool_w, (mi,), n_groups=len(POOL_WINDOWS), col_scale=pool_scale[mi].reshape(1, d))
        if i % 2 == 0:
            x, h = _resid_norm(lay, x, mix, g4, mods)
            hid = _swiglu_hidden(h, ffn_w_gu, (fi,))
            f = _mm(hid, ffn_w_down, (fi,), tn=_pick(d, (256, 128)), a_single_buffer=True)
        else:
            router = jnp.pad(moe_router[fi], ((0, 0), (0, LANES - n_experts)))
            x, h, route = _resid_norm(lay, x, mix, g4, mods, router=router, n_experts=n_experts)
            tm = _pick(TOP_K * lay.m, (MOE_ROW_TILE, 256, 128, 64, 32, 16, 8))
            row_token, tile_expert, n_used, pos = _moe_plan(route, n_experts, tm)
            xs = _moe_gather(h, row_token, n_used, tm)
            hid = _moe_swiglu(xs, moe_w_gu, fi, tile_expert, n_used, tm)
            ys = _moe_down(hid, moe_w_down, fi, tile_expert, n_used, tm)
            x = _resid_moe(lay, x, ys, pos, route, g4, mods)
            continue
        x = _resid(lay, x, f, g4, mods)

    y_prompt = x[:lay.mp].reshape(nb, seq, d)
    y_sample = x[lay.mp:].reshape(ndb, dseq, d)
    return (y_prompt, y_sample, jnp.stack(new_k, axis=1), jnp.stack(new_v, axis=1), jnp.stack(new_s, axis=1))
```

```python
import functools

import numpy as np
import jax
import jax.numpy as jnp
from jax import lax
from jax.experimental import pallas as pl
from jax.experimental.pallas import tpu as pltpu

F32 = jnp.float32
BF16 = jnp.bfloat16
HIGHEST = lax.Precision.HIGHEST

V7X_VMEM_LIMIT_BYTES = 56 * 1024 * 1024
RWKV_VMEM_BUDGET_BYTES = 46 * 1024 * 1024
LANES = 128

NORM_EPS = 1e-6
NEG_INF = -1e30
ATTN_HEAD_DIM = 128
GRID_W = 64
MAX_WIN_ROWS = 8
WIN_COLS = 16
ATTN_LAT_ROW_GROUP = 4
RWKV_HEAD_DIM = 64
RWKV_CHUNK = 64
LNX_EPS = 64e-5
POOL_WINDOWS = (2, 4, 8, 16)
POOL_HALO = 8
TOP_K = 2
TOK_BLOCK = 256


def _cparams(*sem):
    return pltpu.CompilerParams(dimension_semantics=sem, vmem_limit_bytes=V7X_VMEM_LIMIT_BYTES)


def _pick(n, candidates):
    for c in candidates:
        if c <= n and n % c == 0:
            return c
    return n


def _dot(a, b, precision=None):
    return jnp.dot(a, b, preferred_element_type=F32, precision=precision)


def _dot_nt(a, b, precision=None):
    return lax.dot_general(a, b, (((1,), (1,)), ((), ())), preferred_element_type=F32, precision=precision)


def _rms(x, g):
    return x * lax.rsqrt(jnp.mean(x * x, axis=-1, keepdims=True) + NORM_EPS) * g


def _silu(x):
    return x * jax.nn.sigmoid(x)


class _Layout:
    def __init__(self, n_prompt_seq, prompt_len, n_latent_seq, latent_len, tb):
        assert prompt_len % tb == 0 and latent_len % tb == 0
        self.tb = tb
        self.mp = n_prompt_seq * prompt_len
        self.ms = n_latent_seq * latent_len
        self.m = self.mp + self.ms
        self.prompt_len, self.latent_len = prompt_len, latent_len
        self.n_prompt_seq, self.n_latent_seq = n_prompt_seq, n_latent_seq
        rid, first, last = [], [], []
        for i in range(self.m // tb):
            row = i * tb
            if row < self.mp:
                rid.append(0)
                first.append(int(row % prompt_len == 0))
                last.append(int((row + tb) % prompt_len == 0))
            else:
                rid.append(1 + (row - self.mp) // latent_len)
                first.append(int((row - self.mp) % latent_len == 0))
                last.append(int((row - self.mp + tb) % latent_len == 0))
        self.meta = jnp.asarray(np.array([rid, first, last], np.int32))
        self.nblk = self.m // tb


def _ada_kernel(c_ref, w_ref, b_ref, o_ref):
    s = _silu(c_ref[...]).astype(BF16)
    o_ref[...] = _dot(s, w_ref[...].astype(BF16)) + b_ref[...]


def _ada_all(cond8, ada_w, ada_b):
    depth, d, n = ada_w.shape
    tn = _pick(n, (1024, 512, 256, 128))
    return pl.pallas_call(
        _ada_kernel,
        grid=(depth, n // tn),
        in_specs=[pl.BlockSpec((8, d), lambda l, j: (0, 0)),
                  pl.BlockSpec((None, d, tn), lambda l, j: (l, 0, j)),
                  pl.BlockSpec((None, 1, tn), lambda l, j: (l, 0, j))],
        out_specs=pl.BlockSpec((None, 8, tn), lambda l, j: (l, 0, j)),
        out_shape=jax.ShapeDtypeStruct((depth, 8, n), F32),
        compiler_params=_cparams("parallel", "parallel"),
        name="ada",
    )(cond8, ada_w, ada_b.reshape(depth, 1, n))


def _modulated(x, g, mod_ref, shift_row, scale_row):
    return _rms(x, g) * (1.0 + mod_ref[scale_row:scale_row + 1, :]) + mod_ref[shift_row:shift_row + 1, :]


def _router_route(h, rw_ref, n_experts):
    logits = _dot(h, rw_ref[...], precision=HIGHEST)
    lane = lax.broadcasted_iota(jnp.int32, logits.shape, 1)
    valid = lane < n_experts
    logits = jnp.where(valid, logits, NEG_INF)
    e = jnp.exp(logits - jnp.max(logits, axis=-1, keepdims=True))
    p = e / jnp.sum(e, axis=-1, keepdims=True)
    p = jnp.where(valid, p, -2.0)
    m1 = jnp.max(p, axis=-1, keepdims=True)
    i1 = jnp.min(jnp.where(p == m1, lane, LANES), axis=-1, keepdims=True)
    p2 = jnp.where(lane == i1, -1.0, p)
    m2 = jnp.max(p2, axis=-1, keepdims=True)
    i2 = jnp.min(jnp.where(p2 == m2, lane, LANES), axis=-1, keepdims=True)
    den = m1 + m2
    return (jnp.where(lane == 0, i1.astype(F32), 0.0) + jnp.where(lane == 1, i2.astype(F32), 0.0)
            + jnp.where(lane == 2, m1 / den, 0.0) + jnp.where(lane == 3, m2 / den, 0.0))


def _norm_mod_kernel(meta_ref, x_ref, g_ref, mod_ref, o_ref, *, g_row, shift_row, scale_row):
    h = _modulated(x_ref[...], g_ref[g_row:g_row + 1, :], mod_ref, shift_row, scale_row)
    o_ref[...] = h.astype(o_ref.dtype)


def _norm_mod(lay, x, g4, mods, *, g_row, shift_row, scale_row):
    m, d = x.shape
    tb = lay.tb
    return pl.pallas_call(
        functools.partial(_norm_mod_kernel, g_row=g_row, shift_row=shift_row, scale_row=scale_row),
        grid_spec=pltpu.PrefetchScalarGridSpec(
            num_scalar_prefetch=1, grid=(lay.nblk,),
            in_specs=[pl.BlockSpec((tb, d), lambda i, mt: (i, 0)),
                      pl.BlockSpec((4, d), lambda i, mt: (0, 0)),
                      pl.BlockSpec((None, 6, d), lambda i, mt: (mt[0, i], 0, 0))],
            out_specs=pl.BlockSpec((tb, d), lambda i, mt: (i, 0))),
        out_shape=jax.ShapeDtypeStruct((m, d), BF16),
        compiler_params=_cparams("parallel"),
        name="norm_mod",
    )(lay.meta, x, g4, mods)


def _resid_norm_kernel(meta_ref, x_ref, mix_ref, g_ref, mod_ref, *rest, n_experts):
    if n_experts:
        rw_ref, xo_ref, h_ref, route_ref = rest
    else:
        xo_ref, h_ref = rest
    x = x_ref[...] + mod_ref[2:3, :] * _rms(mix_ref[...], g_ref[1:2, :])
    xo_ref[...] = x
    h = _modulated(x, g_ref[2:3, :], mod_ref, 3, 4)
    h_ref[...] = h.astype(h_ref.dtype)
    if n_experts:
        route_ref[...] = _router_route(h, rw_ref, n_experts)


def _resid_norm(lay, x, mix, g4, mods, router=None, n_experts=0):
    m, d = x.shape
    tb = lay.tb
    row = lambda i, mt: (i, 0)
    in_specs = [pl.BlockSpec((tb, d), row), pl.BlockSpec((tb, d), row),
                pl.BlockSpec((4, d), lambda i, mt: (0, 0)),
                pl.BlockSpec((None, 6, d), lambda i, mt: (mt[0, i], 0, 0))]
    out_specs = [pl.BlockSpec((tb, d), row), pl.BlockSpec((tb, d), row)]
    out_shape = [jax.ShapeDtypeStruct((m, d), F32), jax.ShapeDtypeStruct((m, d), F32 if n_experts else BF16)]
    args = [lay.meta, x, mix, g4, mods]
    if n_experts:
        in_specs.append(pl.BlockSpec((d, LANES), lambda i, mt: (0, 0)))
        out_specs.append(pl.BlockSpec((tb, LANES), row))
        out_shape.append(jax.ShapeDtypeStruct((m, LANES), F32))
        args.append(router)
    return pl.pallas_call(
        functools.partial(_resid_norm_kernel, n_experts=n_experts),
        grid_spec=pltpu.PrefetchScalarGridSpec(
            num_scalar_prefetch=1, grid=(lay.nblk,), in_specs=in_specs, out_specs=out_specs),
        out_shape=out_shape,
        compiler_params=_cparams("parallel"),
        name="resid_norm",
    )(*args)


def _resid_kernel(meta_ref, x_ref, f_ref, g_ref, mod_ref, xo_ref):
    xo_ref[...] = x_ref[...] + mod_ref[5:6, :] * _rms(f_ref[...], g_ref[3:4, :])


def _resid(lay, x, f, g4, mods):
    m, d = x.shape
    tb = lay.tb
    row = lambda i, mt: (i, 0)
    return pl.pallas_call(
        _resid_kernel,
        grid_spec=pltpu.PrefetchScalarGridSpec(
            num_scalar_prefetch=1, grid=(lay.nblk,),
            in_specs=[pl.BlockSpec((tb, d), row), pl.BlockSpec((tb, d), row),
                      pl.BlockSpec((4, d), lambda i, mt: (0, 0)),
                      pl.BlockSpec((None, 6, d), lambda i, mt: (mt[0, i], 0, 0))],
            out_specs=pl.BlockSpec((tb, d), row)),
        out_shape=jax.ShapeDtypeStruct((m, d), F32),
        compiler_params=_cparams("parallel"),
        name="resid",
    )(lay.meta, x, f, g4, mods)


def _mm_kernel(a_ref, w_ref, *rest, act, has_scale):
    if has_scale:
        s_ref, o_ref = rest
    else:
        (o_ref,) = rest
    a = a_ref[...]
    if act == "tanh":
        a = jnp.tanh(a)
    elif act == "sigmoid":
        a = jax.nn.sigmoid(a)
    out = _dot(a.astype(BF16), w_ref[...].astype(BF16))
    if has_scale:
        out = out * s_ref[...]
    o_ref[...] = out.astype(o_ref.dtype)


def _mm(a, w, lead=(), *, out_dtype=F32, act=None, a_col=0, n_groups=1, col_scale=None, tm=None, tn=None,
        a_single_buffer=False):
    m = a.shape[0]
    k, n = w.shape[-2:]
    tm = tm or _pick(m, (1024, 512, 256, 128, 64, 32, 16, 8))
    tn = tn or _pick(n, (512, 256, 128))
    nl = len(lead)
    if n_groups > 1:
        grid = (m // tm, n_groups, n // tn)
        a_spec = pl.BlockSpec((tm, k), lambda i, g, j: (i, g))
        w_spec = pl.BlockSpec((None,) * (nl + 1) + (k, tn), lambda i, g, j: lead + (g, 0, j))
        o_spec = pl.BlockSpec((tm, tn), lambda i, g, j: (i, g * (n // tn) + j))
        s_spec = pl.BlockSpec((1, tn), lambda i, g, j: (0, g * (n // tn) + j))
        sem = ("parallel", "arbitrary", "arbitrary")
    else:
        grid = (m // tm, n // tn)
        a_spec = pl.BlockSpec((tm, k), lambda i, j: (i, a_col),
                              **({"pipeline_mode": pl.Buffered(1)} if a_single_buffer else {}))
        w_spec = pl.BlockSpec((None,) * nl + (k, tn), lambda i, j: lead + (0, j))
        o_spec = pl.BlockSpec((tm, tn), lambda i, j: (i, j))
        s_spec = pl.BlockSpec((1, tn), lambda i, j: (0, j))
        sem = ("parallel", "arbitrary")
    in_specs, args = [a_spec, w_spec], [a, w]
    if col_scale is not None:
        in_specs.append(s_spec)
        args.append(col_scale)
    return pl.pallas_call(
        functools.partial(_mm_kernel, act=act, has_scale=col_scale is not None),
        grid=grid, in_specs=in_specs, out_specs=o_spec,
        out_shape=jax.ShapeDtypeStruct((m, n * n_groups), out_dtype),
        compiler_params=_cparams(*sem),
        name="mm",
    )(*args)


def _swiglu_kernel(a_ref, wg_ref, wu_ref, o_ref):
    a = a_ref[...]
    g = _dot(a, wg_ref[...].astype(BF16))
    u = _dot(a, wu_ref[...].astype(BF16))
    o_ref[...] = (_silu(g) * u).astype(o_ref.dtype)


def _swiglu_hidden(a, w_gu, lead):
    m, d = a.shape
    f = w_gu.shape[-1] // 2
    tm = _pick(m, (1024, 512, 256, 128, 64, 32, 16, 8))
    tn = _pick(f, (256, 128))
    nf = f // tn
    wlead = (None,) * len(lead)
    return pl.pallas_call(
        _swiglu_kernel,
        grid=(m // tm, nf),
        in_specs=[pl.BlockSpec((tm, d), lambda i, j: (i, 0)),
                  pl.BlockSpec(wlead + (d, tn), lambda i, j: lead + (0, j)),
                  pl.BlockSpec(wlead + (d, tn), lambda i, j: lead + (0, nf + j))],
        out_specs=pl.BlockSpec((tm, tn), lambda i, j: (i, j)),
        out_shape=jax.ShapeDtypeStruct((m, f), BF16),
        compiler_params=_cparams("parallel", "arbitrary"),
        name="swiglu",
    )(a, w_gu, w_gu)


MOE_ROW_TILE = 768
MOE_GATHER_ROWS = 256


def _moe_plan(route, n_experts, tm):
    m = route.shape[0]
    ids = route[:, :TOP_K].astype(jnp.int32).reshape(-1)
    onehot = (ids[:, None] == jnp.arange(n_experts, dtype=jnp.int32)[None, :]).astype(jnp.int32)
    csum = jnp.cumsum(onehot, axis=0)
    counts = csum[-1]
    rank = jnp.sum(onehot * csum, axis=1) - 1
    tiles_per = (counts + tm - 1) // tm
    tile_end = jnp.cumsum(tiles_per)
    row_start = (tile_end - tiles_per) * tm
    pos = jnp.sum(onehot * row_start[None, :], axis=1) + rank
    n_tiles = (TOP_K * m) // tm + n_experts
    row_token = jnp.zeros((n_tiles * tm,), jnp.int32).at[pos].set(jnp.arange(TOP_K * m, dtype=jnp.int32) // TOP_K)
    t = jnp.arange(n_tiles, dtype=jnp.int32)
    n_used = tile_end[-1]
    tile_expert = jnp.sum((t[:, None] >= tile_end[None, :]).astype(jnp.int32), axis=1)
    last_expert = jnp.sum((n_used - 1 >= tile_end).astype(jnp.int32))
    tile_expert = jnp.where(t < n_used, tile_expert, last_expert)
    return row_token, tile_expert, n_used.reshape(1), pos.reshape(m, TOP_K)


def _moe_gather_kernel(nu_ref, tok_ref, tok_next_ref, h_hbm, o_ref, buf, sem, *, tiles_per_row_tile):
    i = pl.program_id(0)
    tg = buf.shape[1]
    n_active = nu_ref[0] * tiles_per_row_tile
    slot = i % 2

    def copy(idx_ref, k, s):
        return pltpu.make_async_copy(h_hbm.at[pl.ds(idx_ref[0, k], 1), :], buf.at[s, pl.ds(k, 1), :], sem.at[s])

    def start_tile(idx_ref, s):
        def start(k2, carry):
            copy(idx_ref, 2 * k2, s).start(priority=0)
            copy(idx_ref, 2 * k2 + 1, s).start(priority=1)
            return carry

        lax.fori_loop(0, tg // 2, start, 0)

    @pl.when(i == 0)
    def _():
        start_tile(tok_ref, 0)

    @pl.when(i + 1 < n_active)
    def _():
        start_tile(tok_next_ref, 1 - slot)

    @pl.when(i < n_active)
    def _():
        def wait(k, carry):
            copy(tok_ref, k, slot).wait()
            return carry

        lax.fori_loop(0, tg, wait, 0)
        o_ref[...] = buf[slot].astype(o_ref.dtype)

    @pl.when(i >= n_active)
    def _():
        o_ref[...] = jnp.zeros_like(o_ref)


def _moe_gather(h, row_token, n_used, tm):
    m, d = h.shape
    rows = row_token.shape[0]
    tg = _pick(tm, (MOE_GATHER_ROWS, 128, 64, 32, 16, 8))
    nt = rows // tg
    tok = row_token.reshape(nt, 1, tg)
    return pl.pallas_call(
        functools.partial(_moe_gather_kernel, tiles_per_row_tile=tm // tg),
        grid_spec=pltpu.PrefetchScalarGridSpec(
            num_scalar_prefetch=1, grid=(nt,),
            in_specs=[pl.BlockSpec((None, 1, tg), lambda i, nu: (i, 0, 0), memory_space=pltpu.SMEM),
                      pl.BlockSpec((None, 1, tg), lambda i, nu: (jnp.minimum(i + 1, nt - 1), 0, 0),
                                   memory_space=pltpu.SMEM),
                      pl.BlockSpec(memory_space=pl.ANY)],
            out_specs=pl.BlockSpec((tg, d), lambda i, nu: (i, 0)),
            scratch_shapes=[pltpu.VMEM((2, tg, d), F32), pltpu.SemaphoreType.DMA((2,))]),
        out_shape=jax.ShapeDtypeStruct((rows, d), BF16),
        compiler_params=_cparams("arbitrary"),
        name="moe_gather",
    )(n_used, tok, tok, h)


def _moe_swiglu_kernel(te_ref, nu_ref, a_ref, wg_ref, wu_ref, o_ref):
    @pl.when(pl.program_id(0) < nu_ref[0])
    def _():
        a = a_ref[...]
        g = _dot(a, wg_ref[...].astype(BF16))
        u = _dot(a, wu_ref[...].astype(BF16))
        o_ref[...] = (_silu(g) * u).astype(o_ref.dtype)

    @pl.when(pl.program_id(0) >= nu_ref[0])
    def _():
        o_ref[...] = jnp.zeros_like(o_ref)


def _moe_swiglu(a, w_gu, fi, tile_expert, n_used, tm):
    rows, d = a.shape
    f = w_gu.shape[-1] // 2
    tn = _pick(f, (256, 128))
    nf = f // tn
    return pl.pallas_call(
        _moe_swiglu_kernel,
        grid_spec=pltpu.PrefetchScalarGridSpec(
            num_scalar_prefetch=2, grid=(rows // tm, nf),
            in_specs=[pl.BlockSpec((tm, d), lambda t, j, te, nu: (t, 0)),
                      pl.BlockSpec((None, None, d, tn), lambda t, j, te, nu: (fi, te[t], 0, j)),
                      pl.BlockSpec((None, None, d, tn), lambda t, j, te, nu: (fi, te[t], 0, nf + j))],
            out_specs=pl.BlockSpec((tm, tn), lambda t, j, te, nu: (t, j))),
        out_shape=jax.ShapeDtypeStruct((rows, f), BF16),
        compiler_params=_cparams("parallel", "arbitrary"),
        name="moe_swiglu",
    )(tile_expert, n_used, a, w_gu, w_gu)


def _moe_down_kernel(te_ref, nu_ref, a_ref, w_ref, o_ref):
    @pl.when(pl.program_id(0) < nu_ref[0])
    def _():
        o_ref[...] = _dot(a_ref[...], w_ref[...].astype(BF16))

    @pl.when(pl.program_id(0) >= nu_ref[0])
    def _():
        o_ref[...] = jnp.zeros_like(o_ref)


def _moe_down(a, w_down, fi, tile_expert, n_used, tm):
    rows, f = a.shape
    d = w_down.shape[-1]
    tn = _pick(d, (512, 256, 128))
    return pl.pallas_call(
        _moe_down_kernel,
        grid_spec=pltpu.PrefetchScalarGridSpec(
            num_scalar_prefetch=2, grid=(rows // tm, d // tn),
            in_specs=[pl.BlockSpec((tm, f), lambda t, j, te, nu: (t, 0)),
                      pl.BlockSpec((None, None, f, tn), lambda t, j, te, nu: (fi, te[t], 0, j))],
            out_specs=pl.BlockSpec((tm, tn), lambda t, j, te, nu: (t, j))),
        out_shape=jax.ShapeDtypeStruct((rows, d), F32),
        compiler_params=_cparams("parallel", "arbitrary"),
        name="moe_down",
    )(tile_expert, n_used, a, w_down)


def _resid_moe_kernel(meta_ref, pos_ref, pos_next_ref, x_ref, route_ref, g_ref, mod_ref, y_hbm, xo_ref, buf, sem):
    i = pl.program_id(0)
    tb = x_ref.shape[0]
    slot = i % 2

    def copy(idx_ref, k, e, s):
        return pltpu.make_async_copy(y_hbm.at[pl.ds(idx_ref[0, TOP_K * k + e], 1), :],
                                     buf.at[s, e, pl.ds(k, 1), :], sem.at[s])

    def start_tile(idx_ref, s):
        def start(k, carry):
            for e in range(TOP_K):
                copy(idx_ref, k, e, s).start()
            return carry

        lax.fori_loop(0, tb, start, 0)

    @pl.when(i == 0)
    def _():
        start_tile(pos_ref, 0)

    @pl.when(i + 1 < pl.num_programs(0))
    def _():
        start_tile(pos_next_ref, 1 - slot)

    def wait(k, carry):
        for e in range(TOP_K):
            copy(pos_ref, k, e, slot).wait()
        return carry

    lax.fori_loop(0, tb, wait, 0)
    route = route_ref[...]
    f = route[:, 2:3] * buf[slot, 0] + route[:, 3:4] * buf[slot, 1]
    xo_ref[...] = x_ref[...] + mod_ref[5:6, :] * _rms(f, g_ref[3:4, :])


def _resid_moe(lay, x, y_sorted, pos, route, g4, mods):
    m, d = x.shape
    tb = lay.tb
    row = lambda i, mt: (i, 0)
    pos3 = pos.reshape(lay.nblk, 1, TOP_K * tb)
    return pl.pallas_call(
        _resid_moe_kernel,
        grid_spec=pltpu.PrefetchScalarGridSpec(
            num_scalar_prefetch=1, grid=(lay.nblk,),
            in_specs=[pl.BlockSpec((None, 1, TOP_K * tb), lambda i, mt: (i, 0, 0), memory_space=pltpu.SMEM),
                      pl.BlockSpec((None, 1, TOP_K * tb), lambda i, mt: (jnp.minimum(i + 1, lay.nblk - 1), 0, 0),
                                   memory_space=pltpu.SMEM),
                      pl.BlockSpec((tb, d), row), pl.BlockSpec((tb, LANES), row),
                      pl.BlockSpec((4, d), lambda i, mt: (0, 0)),
                      pl.BlockSpec((None, 6, d), lambda i, mt: (mt[0, i], 0, 0)),
                      pl.BlockSpec(memory_space=pl.ANY)],
            out_specs=pl.BlockSpec((tb, d), row),
            scratch_shapes=[pltpu.VMEM((2, TOP_K, tb, d), F32), pltpu.SemaphoreType.DMA((2,))]),
        out_shape=jax.ShapeDtypeStruct((m, d), F32),
        compiler_params=_cparams("arbitrary"),
        name="resid_moe",
    )(lay.meta, pos3, pos3, x, route, g4, mods, y_sorted)


def _softmax_rows(s):
    e = jnp.exp(s - jnp.max(s, axis=-1, keepdims=True))
    return e * (1.0 / jnp.sum(e, axis=-1, keepdims=True))


def _attn_ctx_kernel(q_ref, k_ref, v_ref, _, o_ref, *, heads, scale):
    for h in range(heads):
        sl = slice(h * ATTN_HEAD_DIM, (h + 1) * ATTN_HEAD_DIM)
        s = _dot_nt(q_ref[:, sl].astype(BF16), k_ref[:, sl].astype(BF16)) * scale
        p = _softmax_rows(s).astype(BF16)
        o_ref[:, sl] = _dot(p, v_ref[:, sl].astype(BF16)).astype(o_ref.dtype)


def _attn_context(qkv, n_seq, seq_len, d, o_buf):
    n_heads = d // ATTN_HEAD_DIM
    hb = _pick(n_heads, (4, 2, 1))
    wb = hb * ATTN_HEAD_DIM
    ncb = d // wb
    return pl.pallas_call(
        functools.partial(_attn_ctx_kernel, heads=hb, scale=ATTN_HEAD_DIM ** -0.5),
        grid=(n_seq, ncb),
        in_specs=[pl.BlockSpec((seq_len, wb), lambda b, h: (b, h)),
                  pl.BlockSpec((seq_len, wb), lambda b, h: (b, ncb + h)),
                  pl.BlockSpec((seq_len, wb), lambda b, h: (b, 2 * ncb + h)),
                  pl.BlockSpec(memory_space=pl.ANY)],
        out_specs=pl.BlockSpec((seq_len, wb), lambda b, h: (b, h)),
        out_shape=jax.ShapeDtypeStruct(o_buf.shape, o_buf.dtype),
        input_output_aliases={3: 0},
        compiler_params=_cparams("parallel", "parallel"),
        name="attn_context",
    )(qkv, qkv, qkv, o_buf)


def _attn_lat_kernel(q_ref, k_ref, v_ref, ck_ref, cv_ref, tb_ref, _, o_ref, *, rows, wr, scale):
    kb = k_ref[...].astype(BF16)
    vb = v_ref[...].astype(BF16)
    ckb = ck_ref[...].astype(BF16)
    cvb = cv_ref[...].astype(BF16)
    group = ATTN_LAT_ROW_GROUP if rows % ATTN_LAT_ROW_GROUP == 0 else 1
    row_start = [min(max(r - wr // 2, 0), rows - wr) for r in range(rows)]
    neg = jnp.full((GRID_W, GRID_W), NEG_INF, F32)
    biases = {}
    for r0 in range(0, rows, group):
        qrows = range(r0, r0 + group)
        k0 = row_start[r0]
        k1 = row_start[r0 + group - 1] + wr
        pattern = tuple(tuple(kr - r + MAX_WIN_ROWS - 1 if row_start[r] <= kr < row_start[r] + wr else None
                              for kr in range(k0, k1)) for r in qrows)
        if pattern not in biases:
            biases[pattern] = jnp.concatenate(
                [jnp.concatenate([neg if dr is None else tb_ref[dr] for dr in prow], axis=1) for prow in pattern],
                axis=0)
        q = q_ref[r0 * GRID_W:(r0 + group) * GRID_W, :].astype(BF16)
        kl = kb[k0 * GRID_W:k1 * GRID_W]
        vl = vb[k0 * GRID_W:k1 * GRID_W]
        s_loc = _dot_nt(q, kl) * scale + biases[pattern]
        s_ctx = _dot_nt(q, ckb) * scale
        mx = jnp.maximum(jnp.max(s_loc, axis=-1, keepdims=True), jnp.max(s_ctx, axis=-1, keepdims=True))
        e_loc = jnp.exp(s_loc - mx)
        e_ctx = jnp.exp(s_ctx - mx)
        inv = 1.0 / (jnp.sum(e_loc, axis=-1, keepdims=True) + jnp.sum(e_ctx, axis=-1, keepdims=True))
        o = (_dot(e_loc.astype(BF16), vl) + _dot(e_ctx.astype(BF16), cvb)) * inv
        o_ref[r0 * GRID_W:(r0 + group) * GRID_W, :] = o.astype(o_ref.dtype)


def _rel_bias_table(rpb):
    qc = np.arange(GRID_W)[:, None]
    kc = np.arange(GRID_W)[None, :]
    ws = np.clip(qc - WIN_COLS // 2, 0, GRID_W - WIN_COLS)
    in_win = (kc >= ws) & (kc < ws + WIN_COLS)
    dcol = np.clip(kc - qc, 1 - WIN_COLS, WIN_COLS - 1) + WIN_COLS - 1
    return jnp.where(in_win[None, None], rpb.astype(F32)[:, :, dcol], NEG_INF)


def _attn_latent(qkv, cache_k, cache_v, mi, rpb, row_block0, n_seq, seq_len, d, o_buf):
    n_heads = d // ATTN_HEAD_DIM
    rows = seq_len // GRID_W
    wr = min(MAX_WIN_ROWS, rows)
    past = cache_k.shape[2]
    ck = cache_k.reshape(cache_k.shape[0], cache_k.shape[1], past, d)
    cv = cache_v.reshape(cache_v.shape[0], cache_v.shape[1], past, d)
    table = _rel_bias_table(rpb)
    hd = ATTN_HEAD_DIM
    return pl.pallas_call(
        functools.partial(_attn_lat_kernel, rows=rows, wr=wr, scale=ATTN_HEAD_DIM ** -0.5),
        grid=(n_seq, n_heads),
        in_specs=[pl.BlockSpec((seq_len, hd), lambda b, h: (row_block0 + b, h)),
                  pl.BlockSpec((seq_len, hd), lambda b, h: (row_block0 + b, n_heads + h)),
                  pl.BlockSpec((seq_len, hd), lambda b, h: (row_block0 + b, 2 * n_heads + h)),
                  pl.BlockSpec((None, None, past, hd), lambda b, h: (b, mi, 0, h)),
                  pl.BlockSpec((None, None, past, hd), lambda b, h: (b, mi, 0, h)),
                  pl.BlockSpec((None, 2 * MAX_WIN_ROWS - 1, GRID_W, GRID_W), lambda b, h: (h, 0, 0, 0)),
                  pl.BlockSpec(memory_space=pl.ANY)],
        out_specs=pl.BlockSpec((seq_len, hd), lambda b, h: (row_block0 + b, h)),
        out_shape=jax.ShapeDtypeStruct(o_buf.shape, o_buf.dtype),
        input_output_aliases={6: 0},
        compiler_params=_cparams("parallel", "parallel"),
        name="attn_latent",
    )(qkv, qkv, qkv, ck, cv, table, o_buf)


def _halo_rows(meta_ref, h_prev_blk, h_next_blk):
    i = pl.program_id(0)
    hp = jnp.where(meta_ref[1, i] == 1, 0.0, h_prev_blk[POOL_HALO - 1:POOL_HALO, :])
    hn = jnp.where(meta_ref[2, i] == 1, 0.0, h_next_blk[0:1, :])
    return hp, hn


def _rwkv_prep_kernel(meta_ref, x_ref, xp_ref, xn_ref, g_ref, mod_ref, mu_ref, *o_refs):
    g = g_ref[0:1, :]
    h = _modulated(x_ref[...], g, mod_ref, 0, 1)
    hp, hn = _halo_rows(meta_ref, _modulated(xp_ref[...], g, mod_ref, 0, 1),
                        _modulated(xn_ref[...], g, mod_ref, 0, 1))
    tb = h.shape[0]
    row = lax.broadcasted_iota(jnp.int32, h.shape, 0)
    prev = jnp.where(row == 0, hp, pltpu.roll(h, 1, 0))
    nxt = jnp.where(row == tb - 1, hn, pltpu.roll(h, tb - 1, 0))
    xx = 0.5 * (prev + nxt) - h
    for n, o_ref in enumerate(o_refs):
        o_ref[...] = (h + xx * mu_ref[n:n + 1, :]).astype(o_ref.dtype)


def _halo_specs(lay, d):
    tb = lay.tb
    per = tb // POOL_HALO
    last = lay.nblk * per - 1
    return [pl.BlockSpec((tb, d), lambda i, mt: (i, 0)),
            pl.BlockSpec((POOL_HALO, d), lambda i, mt: (jnp.maximum(i * per - 1, 0), 0)),
            pl.BlockSpec((POOL_HALO, d), lambda i, mt: (jnp.minimum((i + 1) * per, last), 0))]


def _rwkv_prep(lay, x, g4, mods, mu):
    m, d = x.shape
    tb = lay.tb
    return pl.pallas_call(
        _rwkv_prep_kernel,
        grid_spec=pltpu.PrefetchScalarGridSpec(
            num_scalar_prefetch=1, grid=(lay.nblk,),
            in_specs=_halo_specs(lay, d) + [
                pl.BlockSpec((4, d), lambda i, mt: (0, 0)),
                pl.BlockSpec((None, 6, d), lambda i, mt: (mt[0, i], 0, 0)),
                pl.BlockSpec((6, d), lambda i, mt: (0, 0))],
            out_specs=[pl.BlockSpec((tb, d), lambda i, mt: (i, 0))] * 6),
        out_shape=[jax.ShapeDtypeStruct((m, d), BF16)] * 6,
        compiler_params=_cparams("parallel"),
        name="rwkv_prep",
    )(lay.meta, x, x, x, g4, mods, mu)


def _split(x):
    hi = x.astype(BF16)
    return hi, (x - hi.astype(F32)).astype(BF16)


def _dot3(a, b):
    (ah, al), (bh, bl) = a, b
    return _dot(jnp.concatenate([ah, ah, al], axis=1), jnp.concatenate([bh, bl, bh], axis=0))


def _dot3_nt(a, b):
    (ah, al), (bh, bl) = a, b
    return _dot_nt(jnp.concatenate([ah, ah, al], axis=1), jnp.concatenate([bh, bl, bh], axis=1))


def _rwkv_scan_kernel(*refs, seq_len, has_s0, pairs):
    if has_s0:
        (r_ref, k_ref, v_ref, w0_ref, w1_ref, a0_ref, a1_ref, g_ref, pv_ref, s0_ref, _, y_ref, yf_ref, yb_ref) = refs
        sf_ref = None
    else:
        (r_ref, k_ref, v_ref, w0_ref, w1_ref, a0_ref, a1_ref, g_ref, pv_ref, _, y_ref, sf_ref, yf_ref, yb_ref) = refs
        s0_ref = None
    c = RWKV_CHUNK
    n = RWKV_HEAD_DIM
    nchunks = seq_len // c
    wpre_refs = (w0_ref, w1_ref)
    apre_refs = (a0_ref, a1_ref)
    def iota(shape, dim):
        return lax.broadcasted_iota(jnp.int32, shape, dim)

    head0 = iota((c, LANES), 1) < n
    seg = (iota((LANES, LANES), 0) // n == iota((LANES, LANES), 1) // n)
    seg_f = seg.astype(F32)
    seg_b = seg_f.astype(BF16)
    tt = iota((c, 2 * c), 0)
    ss = iota((c, 2 * c), 1) % c
    lo_half = iota((c, 2 * c), 1) < c
    strict = (ss < tt, ss > tt)
    incl = (ss <= tt, ss >= tt)
    t2 = iota((c, c), 0)
    s2 = iota((c, c), 1)
    tri = ((s2 <= t2).astype(F32).astype(BF16), (s2 >= t2).astype(F32).astype(BF16))

    def seg_sum(x):
        hi, lo = _split(x)
        return _dot(jnp.concatenate([hi, lo], axis=1), jnp.concatenate([seg_b, seg_b], axis=0))

    head0_b = head0.astype(F32).astype(BF16)
    head1_b = (1.0 - head0.astype(F32)).astype(BF16)
    lo_half_b = lo_half.astype(F32).astype(BF16)
    hi_half_b = (1.0 - lo_half.astype(F32)).astype(BF16)

    def stack2(x):
        return jnp.concatenate([x * head0_b, x * head1_b], axis=0)

    def stack2_parts(parts):
        return tuple(stack2(p) for p in parts)

    def blockdiag(x):
        return jnp.concatenate([x * lo_half_b, x * hi_half_b], axis=0)

    def cat0(a, b):
        return tuple(jnp.concatenate([x, y], axis=0) for x, y in zip(a, b))

    def features(rows, lanes, d):
        k = k_ref[rows, lanes]
        kk = k * pv_ref[0:1, lanes]
        kk = kk / jnp.maximum(jnp.sqrt(seg_sum(kk * kk)), 1e-12)
        wx = -(pv_ref[5 + d:6 + d, lanes] + wpre_refs[d][rows, lanes])
        w_log = -(jnp.maximum(wx, 0.0) + jnp.log(1.0 + jnp.exp(-jnp.abs(wx)))) - 0.5
        logw = -jnp.exp(w_log)
        a = jax.nn.sigmoid(pv_ref[7 + d:8 + d, lanes] + apre_refs[d][rows, lanes])
        kd = k * (1.0 + (a - 1.0) * pv_ref[1:2, lanes])
        return kk, kd, kk * a, logw

    def chunk(rows, lanes, d, s_bd):
        r = r_ref[rows, lanes]
        v = v_ref[rows, lanes]
        kk, kd, bb, logw = features(rows, lanes, d)
        yield
        l1 = logw.astype(BF16)
        l2 = (logw - l1.astype(F32)).astype(BF16)
        l3 = (logw - l1.astype(F32) - l2.astype(F32)).astype(BF16)
        cum = _dot(jnp.concatenate([tri[d]] * 3, axis=1), jnp.concatenate([l1, l2, l3], axis=0))
        yield
        total = cum[c - 1:c, :] if d == 0 else cum[0:1, :]
        e_inv = jnp.exp(-cum)
        e_rest = jnp.exp(total - cum)
        at = _split(kk * jnp.exp(cum - logw))
        rt = (r * jnp.exp(cum)).astype(BF16)
        kb = cat0(stack2_parts(_split(kd * e_inv)), stack2_parts(_split(bb * e_inv)))
        s_parts = _split(s_bd)
        kbs = cat0(kb, s_parts)
        sa = _dot3_nt(at, kbs)
        sr = _dot_nt(rt, kbs[0])
        yield
        zero = jnp.zeros((c, 2 * c), F32)
        m_cat = jnp.where(strict[d], sa[:, :2 * c], zero)
        l_cat = jnp.where(strict[d], sa[:, 2 * c:4 * c], zero)
        rk_cat = jnp.where(incl[d], sr[:, :2 * c], zero)
        rb_cat = jnp.where(incl[d], sr[:, 2 * c:4 * c], zero)
        y0 = sr[:, 4 * c:]
        v_st = stack2_parts(_split(v))
        x = sa[:, 4 * c:] + _dot3(_split(m_cat), v_st)
        yield
        p = -l_cat
        n_stage = c.bit_length() - 1
        for stage in range(n_stage):
            pp = _split(p)
            xs = stack2_parts(_split(x))
            if stage + 1 < n_stage:
                px = _dot3(pp, tuple(jnp.concatenate([xi, blockdiag(pi)], axis=1) for xi, pi in zip(xs, pp)))
                x = x + px[:, :LANES]
                p = px[:, LANES:]
            else:
                x = x + _dot3(pp, xs)
            yield
        u = x
        u_parts = _split(u)
        y = y0 + _dot(
            jnp.concatenate([rk_cat, rb_cat], axis=1).astype(BF16),
            jnp.concatenate([v_st[0], -stack2(u_parts[0])], axis=0))
        vu_t = _split(jnp.concatenate([v, -u], axis=0).T)
        kb_rest = cat0(_split(kd * e_rest), _split(bb * e_rest))
        s_new = s_bd * jnp.exp(total) + seg_f * _dot3(vu_t, kb_rest)
        return y, s_new

    def run_lockstep(gens):
        results = [None] * len(gens)
        active = list(range(len(gens)))
        while active:
            for i in list(active):
                try:
                    next(gens[i])
                except StopIteration as stop:
                    results[i] = stop.value
                    active.remove(i)
        return results

    chains = [(pi, d) for pi in range(pairs) for d in range(2)]
    if has_s0:
        z = jnp.zeros((n, n), F32)
        s_init = tuple(
            jnp.concatenate([jnp.concatenate([s0_ref[d, 2 * pi], z], axis=1),
                             jnp.concatenate([z, s0_ref[d, 2 * pi + 1]], axis=1)], axis=0) for pi, d in chains)
    else:
        s_init = tuple(jnp.zeros((LANES, LANES), F32) for _ in chains)

    def body(ci, carry):
        rows_fb = (pl.ds(pl.multiple_of(ci * c, c), c), pl.ds(pl.multiple_of((nchunks - 1 - ci) * c, c), c))
        lanes = [slice(pi * LANES, (pi + 1) * LANES) for pi, _ in chains]
        results = run_lockstep([chunk(rows_fb[d], ln, d, s_bd)
                                for (_, d), ln, s_bd in zip(chains, lanes, carry)])
        for (_, d), ln, (y, _) in zip(chains, lanes, results):
            (yf_ref, yb_ref)[d][rows_fb[d], ln] = y
        return tuple(s_bd for _, s_bd in results)

    s_fin = lax.fori_loop(0, nchunks, body, s_init)
    if sf_ref is not None:
        for (pi, d), s_bd in zip(chains, s_fin):
            sf_ref[d, 2 * pi] = s_bd[:n, :n]
            sf_ref[d, 2 * pi + 1] = s_bd[n:, n:]

    fr = 2 * c if seq_len % (2 * c) == 0 else c

    def finish_rows(rows, lanes):
        y = yf_ref[rows, lanes] + yb_ref[rows, lanes]
        a_sum = sum(jax.nn.sigmoid(pv_ref[7 + d:8 + d, lanes] + apre_refs[d][rows, lanes]) for d in range(2))
        kd_sum = k_ref[rows, lanes] * (2.0 + (a_sum - 2.0) * pv_ref[1:2, lanes])
        sums = seg_sum(jnp.concatenate([y, r_ref[rows, lanes] * kd_sum * pv_ref[2:3, lanes]], axis=0))
        yield
        yc = y - sums[:fr] * (1.0 / n)
        var = seg_sum(yc * yc) * (1.0 / n)
        yield
        y = yc * lax.rsqrt(var + LNX_EPS) * pv_ref[3:4, lanes] + pv_ref[4:5, lanes] + sums[fr:] * v_ref[rows, lanes]
        y_ref[rows, lanes] = (y * g_ref[rows, lanes]).astype(y_ref.dtype)

    def finish(ci, carry):
        rows = pl.ds(pl.multiple_of(ci * fr, fr), fr)
        run_lockstep([finish_rows(rows, slice(pi * LANES, (pi + 1) * LANES)) for pi in range(pairs)])
        return carry

    lax.fori_loop(0, seq_len // fr, finish, 0)


def _rwkv_scan(r, k, v, wpre, apre, g, pvec, *, row_block0, n_seq, seq_len, s0=None, mi=0, y_buf=None):
    m, d = r.shape
    n = RWKV_HEAD_DIM
    def fits(p, bufs):
        return (d // LANES) % p == 0 and (bufs * 8 + 2 + 1) * seq_len * p * LANES * 4 <= RWKV_VMEM_BUDGET_BYTES

    pairs, bufs = next((p, b) for p in (4, 2, 1) for b in (2, 1) if fits(p, b))
    wb = pairs * LANES
    nblk = d // wb
    seq_map = lambda b, p: (row_block0 + b, p)
    blk = pl.BlockSpec((seq_len, wb), seq_map)
    blk_in = blk if bufs == 2 else pl.BlockSpec((seq_len, wb), seq_map, pipeline_mode=pl.Buffered(1))
    in_specs = [blk_in] * 8 + [pl.BlockSpec((16, wb), lambda b, p: (0, p))]
    args = [r, k, v, wpre[0], wpre[1], apre[0], apre[1], g, pvec]
    y_shape = jax.ShapeDtypeStruct((m, d), BF16)
    if s0 is not None:
        in_specs.append(pl.BlockSpec((None, None, 2, 2 * pairs, n, n), lambda b, p: (b, mi, 0, p, 0, 0)))
        args.append(s0)
    in_specs.append(pl.BlockSpec(memory_space=pl.ANY))
    args.append(y_buf)
    aliases = {len(args) - 1: 0}
    if s0 is not None:
        out_specs, out_shape = blk, y_shape
    else:
        out_specs = [blk, pl.BlockSpec((None, 2, 2 * pairs, n, n), lambda b, p: (b, 0, p, 0, 0))]
        out_shape = [y_shape, jax.ShapeDtypeStruct((n_seq, 2, d // n, n, n), F32)]
    return pl.pallas_call(
        functools.partial(_rwkv_scan_kernel, seq_len=seq_len, has_s0=s0 is not None, pairs=pairs),
        grid=(n_seq, nblk),
        in_specs=in_specs, out_specs=out_specs, out_shape=out_shape, input_output_aliases=aliases,
        scratch_shapes=[pltpu.VMEM((seq_len, wb), F32), pltpu.VMEM((seq_len, wb), F32)],
        compiler_params=_cparams("parallel", "parallel"),
        name="rwkv_scan",
    )(*args)


def _pool_prep_kernel(meta_ref, x_ref, xp_ref, xn_ref, g_ref, mod_ref, o_ref):
    i = pl.program_id(0)
    g = g_ref[0:1, :]
    h = _modulated(x_ref[...], g, mod_ref, 0, 1)
    hp = jnp.where(meta_ref[1, i] == 1, 0.0, _modulated(xp_ref[...], g, mod_ref, 0, 1))
    hn = jnp.where(meta_ref[2, i] == 1, 0.0, _modulated(xn_ref[...], g, mod_ref, 0, 1))
    tb, d = h.shape
    pd = d // len(POOL_WINDOWS)
    ext = jnp.concatenate([hp, h, hn], axis=0)
    ne = tb + 2 * POOL_HALO
    row = lax.broadcasted_iota(jnp.int32, (tb, pd), 0)
    at_first = (meta_ref[1, i] == 1).astype(jnp.int32)
    at_last = (meta_ref[2, i] == 1).astype(jnp.int32)
    acc = ext[:, 0:d] + pltpu.roll(ext, 1, 0)
    half = 1
    for gi, win in enumerate(POOL_WINDOWS):
        if gi > 0:
            sub = acc[:, pd:]
            acc = pltpu.roll(sub, half, 0) + pltpu.roll(sub, ne - half, 0)
            half *= 2
        wsum = acc[POOL_HALO:POOL_HALO + tb, 0:pd]
        missing = (at_first * jnp.maximum(win // 2 - row, 0) + at_last * jnp.maximum(row + win // 2 - tb, 0))
        cnt = (win - missing).astype(F32)
        o_ref[:, gi * pd:(gi + 1) * pd] = (wsum / cnt - h[:, gi * pd:(gi + 1) * pd]).astype(o_ref.dtype)


def _pool_prep(lay, x, g4, mods):
    m, d = x.shape
    tb = lay.tb
    return pl.pallas_call(
        _pool_prep_kernel,
        grid_spec=pltpu.PrefetchScalarGridSpec(
            num_scalar_prefetch=1, grid=(lay.nblk,),
            in_specs=_halo_specs(lay, d) + [
                pl.BlockSpec((4, d), lambda i, mt: (0, 0)),
                pl.BlockSpec((None, 6, d), lambda i, mt: (mt[0, i], 0, 0))],
            out_specs=pl.BlockSpec((tb, d), lambda i, mt: (i, 0))),
        out_shape=jax.ShapeDtypeStruct((m, d), BF16),
        compiler_params=_cparams("parallel"),
        name="pool_prep",
    )(lay.meta, x, x, x, g4, mods)


def kernel(x_prompt, x_sample, cache_k, cache_v, state_rwkv, c, c_ctx, ada_w, ada_b, norm_g,
           attn_w_qkv, attn_w_o, attn_rpb, rwkv_mu, rwkv_w_r, rwkv_w_k, rwkv_w_v, rwkv_w_o,
           rwkv_w0, rwkv_w1, rwkv_w2, rwkv_a0, rwkv_a1, rwkv_a2, rwkv_g1, rwkv_g2, rwkv_k_k, rwkv_k_a,
           rwkv_r_k, rwkv_lnx_w, rwkv_lnx_b, pool_w, pool_scale, ffn_w_gu, ffn_w_down,
           moe_router, moe_w_gu, moe_w_down):
    nb, seq, d = x_prompt.shape
    ndb, dseq, _ = x_sample.shape
    depth = ada_w.shape[0]
    n_experts = moe_router.shape[-1]
    lay = _Layout(nb, seq, ndb, dseq, _pick(np.gcd(seq, dseq), (TOK_BLOCK, 128, 64, 32, 16)))
    lay_prep = _Layout(nb, seq, ndb, dseq, _pick(np.gcd(seq, dseq), (TOK_BLOCK // 2, 64, 32, 16)))
    assert lay.mp % dseq == 0
    lat_block0 = lay.mp // dseq

    x = jnp.concatenate([x_prompt.reshape(lay.mp, d), x_sample.reshape(lay.ms, d)], axis=0)
    cond8 = jnp.zeros((8, d), F32).at[0].set(c_ctx).at[1:1 + ndb].set(c)
    mods_all = _ada_all(cond8, ada_w, ada_b).reshape(depth, 8, 6, d)

    new_k, new_v, new_s = [], [], []
    for i in range(depth):
        kind, mi, fi = i % 3, i // 3, i // 2
        mods = mods_all[i]
        g4 = norm_g[i]
        if kind == 0:
            h = _norm_mod(lay, x, g4, mods, g_row=0, shift_row=0, scale_row=1)
            qkv = _mm(h, attn_w_qkv, (mi,))
            o = _attn_context(qkv, nb, seq, d, jnp.zeros((lay.m, d), BF16))
            o = _attn_latent(qkv, cache_k, cache_v, mi, attn_rpb[mi], lat_block0, ndb, dseq, d, o)
            mix = _mm(o, attn_w_o, (mi,))
            heads = d // ATTN_HEAD_DIM
            new_k.append(qkv[:lay.mp, d:2 * d].reshape(nb, seq, heads, ATTN_HEAD_DIM))
            new_v.append(qkv[:lay.mp, 2 * d:].reshape(nb, seq, heads, ATTN_HEAD_DIM))
        elif kind == 1:
            xr, xw, xk, xv, xa, xg = _rwkv_prep(lay_prep, x, g4, mods, rwkv_mu[mi])
            r = _mm(xr, rwkv_w_r, (mi,))
            k = _mm(xk, rwkv_w_k, (mi,))
            v = _mm(xv, rwkv_w_v, (mi,))
            lw, la = rwkv_w1.shape[-1], rwkv_a1.shape[-1]
            w1cat = jnp.moveaxis(rwkv_w1[mi], 0, 1).reshape(d, 2 * lw)
            a1cat = jnp.moveaxis(rwkv_a1[mi], 0, 1).reshape(d, 2 * la)
            lg = rwkv_g1.shape[-1]
            lgp = -(-lg // LANES) * LANES
            g1p = jnp.pad(rwkv_g1[mi], ((0, 0), (0, lgp - lg)))
            g2p = jnp.pad(rwkv_g2[mi], ((0, lgp - lg), (0, 0)))
            tw = _mm(xw, w1cat)
            ta = _mm(xa, a1cat)
            tg = _mm(xg, g1p)
            wpre = [_mm(tw, rwkv_w2, (mi, dd), act="tanh", a_col=dd) for dd in range(2)]
            apre = [_mm(ta, rwkv_a2, (mi, dd), a_col=dd) for dd in range(2)]
            gate = _mm(tg, g2p, act="sigmoid")
            pvec = jnp.zeros((16, d), F32)
            for row, val in enumerate((rwkv_k_k[mi], rwkv_k_a[mi], rwkv_r_k[mi].reshape(d), rwkv_lnx_w[mi],
                                       rwkv_lnx_b[mi], rwkv_w0[mi, 0], rwkv_w0[mi, 1], rwkv_a0[mi, 0],
                                       rwkv_a0[mi, 1])):
                pvec = pvec.at[row].set(val)
            y, s_p = _rwkv_scan(r, k, v, wpre, apre, gate, pvec, row_block0=0, n_seq=nb, seq_len=seq,
                                y_buf=jnp.zeros((lay.m, d), BF16))
            y = _rwkv_scan(r, k, v, wpre, apre, gate, pvec, row_block0=lat_block0, n_seq=ndb, seq_len=dseq,
                           s0=state_rwkv, mi=mi, y_buf=y)
            mix = _mm(y, rwkv_w_o, (mi,))
            new_s.append(s_p)
        else:
            hd = _pool_prep(lay_prep, x, g4, mods)
            mix = _mm(hd, pool_w, (mi,), n_groups=len(POOL_WINDOWS), col_scale=pool_scale[mi].reshape(1, d))
        if i % 2 == 0:
            x, h = _resid_norm(lay, x, mix, g4, mods)
            hid = _swiglu_hidden(h, ffn_w_gu, (fi,))
            f = _mm(hid, ffn_w_down, (fi,), tn=_pick(d, (256, 128)), a_single_buffer=True)
        else:
            router = jnp.pad(moe_router[fi], ((0, 0), (0, LANES - n_experts)))
            x, h, route = _resid_norm(lay, x, mix, g4, mods, router=router, n_experts=n_experts)
            tm = _pick(TOP_K * lay.m, (MOE_ROW_TILE, 256, 128, 64, 32, 16, 8))
            row_token, tile_expert, n_used, pos = _moe_plan(route, n_experts, tm)
            xs = _moe_gather(h, row_token, n_used, tm)
            hid = _moe_swiglu(xs, moe_w_gu, fi, tile_expert, n_used, tm)
            ys = _moe_down(hid, moe_w_down, fi, tile_expert, n_used, tm)
            x = _resid_moe(lay, x, ys, pos, route, g4, mods)
            continue
        x = _resid(lay, x, f, g4, mods)

    y_prompt = x[:lay.mp].reshape(nb, seq, d)
    y_sample = x[lay.mp:].reshape(ndb, dseq, d)
    return (y_prompt, y_sample, jnp.stack(new_k, axis=1), jnp.stack(new_v, axis=1), jnp.stack(new_s, axis=1))
```

```python
import functools

import numpy as np
import jax
import jax.numpy as jnp
from jax import lax
from jax.experimental import pallas as pl
from jax.experimental.pallas import tpu as pltpu

F32 = jnp.float32
BF16 = jnp.bfloat16
HIGHEST = lax.Precision.HIGHEST

V7X_VMEM_LIMIT_BYTES = 56 * 1024 * 1024
RWKV_VMEM_BUDGET_BYTES = 46 * 1024 * 1024
LANES = 128

NORM_EPS = 1e-6
NEG_INF = -1e30
ATTN_HEAD_DIM = 128
GRID_W = 64
MAX_WIN_ROWS = 8
WIN_COLS = 16
ATTN_LAT_ROW_GROUP = 4
RWKV_HEAD_DIM = 64
RWKV_CHUNK = 64
LNX_EPS = 64e-5
POOL_WINDOWS = (2, 4, 8, 16)
POOL_HALO = 8
TOP_K = 2
TOK_BLOCK = 256


def _cparams(*sem):
    return pltpu.CompilerParams(dimension_semantics=sem, vmem_limit_bytes=V7X_VMEM_LIMIT_BYTES)


def _pick(n, candidates):
    for c in candidates:
        if c <= n and n % c == 0:
            return c
    return n


def _dot(a, b, precision=None):
    return jnp.dot(a, b, preferred_element_type=F32, precision=precision)


def _dot_nt(a, b, precision=None):
    return lax.dot_general(a, b, (((1,), (1,)), ((), ())), preferred_element_type=F32, precision=precision)


def _rms(x, g):
    return x * lax.rsqrt(jnp.mean(x * x, axis=-1, keepdims=True) + NORM_EPS) * g


def _silu(x):
    return x * jax.nn.sigmoid(x)


class _Layout:
    def __init__(self, n_prompt_seq, prompt_len, n_latent_seq, latent_len, tb):
        assert prompt_len % tb == 0 and latent_len % tb == 0
        self.tb = tb
        self.mp = n_prompt_seq * prompt_len
        self.ms = n_latent_seq * latent_len
        self.m = self.mp + self.ms
        self.prompt_len, self.latent_len = prompt_len, latent_len
        self.n_prompt_seq, self.n_latent_seq = n_prompt_seq, n_latent_seq
        rid, first, last = [], [], []
        for i in range(self.m // tb):
            row = i * tb
            if row < self.mp:
                rid.append(0)
                first.append(int(row % prompt_len == 0))
                last.append(int((row + tb) % prompt_len == 0))
            else:
                rid.append(1 + (row - self.mp) // latent_len)
                first.append(int((row - self.mp) % latent_len == 0))
                last.append(int((row - self.mp + tb) % latent_len == 0))
        self.meta = jnp.asarray(np.array([rid, first, last], np.int32))
        self.nblk = self.m // tb


def _ada_kernel(c_ref, w_ref, b_ref, o_ref):
    s = _silu(c_ref[...]).astype(BF16)
    o_ref[...] = _dot(s, w_ref[...].astype(BF16)) + b_ref[...]


def _ada_all(cond8, ada_w, ada_b):
    depth, d, n = ada_w.shape
    tn = _pick(n, (1024, 512, 256, 128))
    return pl.pallas_call(
        _ada_kernel,
        grid=(depth, n // tn),
        in_specs=[pl.BlockSpec((8, d), lambda l, j: (0, 0)),
                  pl.BlockSpec((None, d, tn), lambda l, j: (l, 0, j)),
                  pl.BlockSpec((None, 1, tn), lambda l, j: (l, 0, j))],
        out_specs=pl.BlockSpec((None, 8, tn), lambda l, j: (l, 0, j)),
        out_shape=jax.ShapeDtypeStruct((depth, 8, n), F32),
        compiler_params=_cparams("parallel", "parallel"),
        name="ada",
    )(cond8, ada_w, ada_b.reshape(depth, 1, n))


def _modulated(x, g, mod_ref, shift_row, scale_row):
    return _rms(x, g) * (1.0 + mod_ref[scale_row:scale_row + 1, :]) + mod_ref[shift_row:shift_row + 1, :]


def _router_route(h, rw_ref, n_experts):
    logits = _dot(h, rw_ref[...], precision=HIGHEST)
    lane = lax.broadcasted_iota(jnp.int32, logits.shape, 1)
    valid = lane < n_experts
    logits = jnp.where(valid, logits, NEG_INF)
    e = jnp.exp(logits - jnp.max(logits, axis=-1, keepdims=True))
    p = e / jnp.sum(e, axis=-1, keepdims=True)
    p = jnp.where(valid, p, -2.0)
    m1 = jnp.max(p, axis=-1, keepdims=True)
    i1 = jnp.min(jnp.where(p == m1, lane, LANES), axis=-1, keepdims=True)
    p2 = jnp.where(lane == i1, -1.0, p)
    m2 = jnp.max(p2, axis=-1, keepdims=True)
    i2 = jnp.min(jnp.where(p2 == m2, lane, LANES), axis=-1, keepdims=True)
    den = m1 + m2
    return (jnp.where(lane == 0, i1.astype(F32), 0.0) + jnp.where(lane == 1, i2.astype(F32), 0.0)
            + jnp.where(lane == 2, m1 / den, 0.0) + jnp.where(lane == 3, m2 / den, 0.0))


def _norm_mod_kernel(meta_ref, x_ref, g_ref, mod_ref, o_ref, *, g_row, shift_row, scale_row):
    h = _modulated(x_ref[...], g_ref[g_row:g_row + 1, :], mod_ref, shift_row, scale_row)
    o_ref[...] = h.astype(o_ref.dtype)


def _norm_mod(lay, x, g4, mods, *, g_row, shift_row, scale_row):
    m, d = x.shape
    tb = lay.tb
    return pl.pallas_call(
        functools.partial(_norm_mod_kernel, g_row=g_row, shift_row=shift_row, scale_row=scale_row),
        grid_spec=pltpu.PrefetchScalarGridSpec(
            num_scalar_prefetch=1, grid=(lay.nblk,),
            in_specs=[pl.BlockSpec((tb, d), lambda i, mt: (i, 0)),
                      pl.BlockSpec((4, d), lambda i, mt: (0, 0)),
                      pl.BlockSpec((None, 6, d), lambda i, mt: (mt[0, i], 0, 0))],
            out_specs=pl.BlockSpec((tb, d), lambda i, mt: (i, 0))),
        out_shape=jax.ShapeDtypeStruct((m, d), BF16),
        compiler_params=_cparams("parallel"),
        name="norm_mod",
    )(lay.meta, x, g4, mods)


def _resid_norm_kernel(meta_ref, x_ref, mix_ref, g_ref, mod_ref, *rest, n_experts):
    if n_experts:
        rw_ref, xo_ref, h_ref, route_ref = rest
    else:
        xo_ref, h_ref = rest
    x = x_ref[...] + mod_ref[2:3, :] * _rms(mix_ref[...], g_ref[1:2, :])
    xo_ref[...] = x
    h = _modulated(x, g_ref[2:3, :], mod_ref, 3, 4)
    h_ref[...] = h.astype(h_ref.dtype)
    if n_experts:
        route_ref[...] = _router_route(h, rw_ref, n_experts)


def _resid_norm(lay, x, mix, g4, mods, router=None, n_experts=0):
    m, d = x.shape
    tb = lay.tb
    row = lambda i, mt: (i, 0)
    in_specs = [pl.BlockSpec((tb, d), row), pl.BlockSpec((tb, d), row),
                pl.BlockSpec((4, d), lambda i, mt: (0, 0)),
                pl.BlockSpec((None, 6, d), lambda i, mt: (mt[0, i], 0, 0))]
    out_specs = [pl.BlockSpec((tb, d), row), pl.BlockSpec((tb, d), row)]
    out_shape = [jax.ShapeDtypeStruct((m, d), F32), jax.ShapeDtypeStruct((m, d), F32 if n_experts else BF16)]
    args = [lay.meta, x, mix, g4, mods]
    if n_experts:
        in_specs.append(pl.BlockSpec((d, LANES), lambda i, mt: (0, 0)))
        out_specs.append(pl.BlockSpec((tb, LANES), row))
        out_shape.append(jax.ShapeDtypeStruct((m, LANES), F32))
        args.append(router)
    return pl.pallas_call(
        functools.partial(_resid_norm_kernel, n_experts=n_experts),
        grid_spec=pltpu.PrefetchScalarGridSpec(
            num_scalar_prefetch=1, grid=(lay.nblk,), in_specs=in_specs, out_specs=out_specs),
        out_shape=out_shape,
        compiler_params=_cparams("parallel"),
        name="resid_norm",
    )(*args)


def _resid_kernel(meta_ref, x_ref, f_ref, g_ref, mod_ref, xo_ref):
    xo_ref[...] = x_ref[...] + mod_ref[5:6, :] * _rms(f_ref[...], g_ref[3:4, :])


def _resid(lay, x, f, g4, mods):
    m, d = x.shape
    tb = lay.tb
    row = lambda i, mt: (i, 0)
    return pl.pallas_call(
        _resid_kernel,
        grid_spec=pltpu.PrefetchScalarGridSpec(
            num_scalar_prefetch=1, grid=(lay.nblk,),
            in_specs=[pl.BlockSpec((tb, d), row), pl.BlockSpec((tb, d), row),
                      pl.BlockSpec((4, d), lambda i, mt: (0, 0)),
                      pl.BlockSpec((None, 6, d), lambda i, mt: (mt[0, i], 0, 0))],
            out_specs=pl.BlockSpec((tb, d), row)),
        out_shape=jax.ShapeDtypeStruct((m, d), F32),
        compiler_params=_cparams("parallel"),
        name="resid",
    )(lay.meta, x, f, g4, mods)


def _mm_kernel(a_ref, w_ref, *rest, act, has_scale):
    if has_scale:
        s_ref, o_ref = rest
    else:
        (o_ref,) = rest
    a = a_ref[...]
    if act == "tanh":
        a = jnp.tanh(a)
    elif act == "sigmoid":
        a = jax.nn.sigmoid(a)
    out = _dot(a.astype(BF16), w_ref[...].astype(BF16))
    if has_scale:
        out = out * s_ref[...]
    o_ref[...] = out.astype(o_ref.dtype)


def _mm(a, w, lead=(), *, out_dtype=F32, act=None, a_col=0, n_groups=1, col_scale=None, tm=None, tn=None,
        a_single_buffer=False):
    m = a.shape[0]
    k, n = w.shape[-2:]
    tm = tm or _pick(m, (1024, 512, 256, 128, 64, 32, 16, 8))
    tn = tn or _pick(n, (512, 256, 128))
    nl = len(lead)
    if n_groups > 1:
        grid = (m // tm, n_groups, n // tn)
        a_spec = pl.BlockSpec((tm, k), lambda i, g, j: (i, g))
        w_spec = pl.BlockSpec((None,) * (nl + 1) + (k, tn), lambda i, g, j: lead + (g, 0, j))
        o_spec = pl.BlockSpec((tm, tn), lambda i, g, j: (i, g * (n // tn) + j))
        s_spec = pl.BlockSpec((1, tn), lambda i, g, j: (0, g * (n // tn) + j))
        sem = ("parallel", "arbitrary", "arbitrary")
    else:
        grid = (m // tm, n // tn)
        a_spec = pl.BlockSpec((tm, k), lambda i, j: (i, a_col),
                              **({"pipeline_mode": pl.Buffered(1)} if a_single_buffer else {}))
        w_spec = pl.BlockSpec((None,) * nl + (k, tn), lambda i, j: lead + (0, j))
        o_spec = pl.BlockSpec((tm, tn), lambda i, j: (i, j))
        s_spec = pl.BlockSpec((1, tn), lambda i, j: (0, j))
        sem = ("parallel", "arbitrary")
    in_specs, args = [a_spec, w_spec], [a, w]
    if col_scale is not None:
        in_specs.append(s_spec)
        args.append(col_scale)
    return pl.pallas_call(
        functools.partial(_mm_kernel, act=act, has_scale=col_scale is not None),
        grid=grid, in_specs=in_specs, out_specs=o_spec,
        out_shape=jax.ShapeDtypeStruct((m, n * n_groups), out_dtype),
        compiler_params=_cparams(*sem),
        name="mm",
    )(*args)


def _swiglu_kernel(a_ref, wg_ref, wu_ref, o_ref):
    a = a_ref[...]
    g = _dot(a, wg_ref[...].astype(BF16))
    u = _dot(a, wu_ref[...].astype(BF16))
    o_ref[...] = (_silu(g) * u).astype(o_ref.dtype)


def _swiglu_hidden(a, w_gu, lead):
    m, d = a.shape
    f = w_gu.shape[-1] // 2
    tm = _pick(m, (1024, 512, 256, 128, 64, 32, 16, 8))
    tn = _pick(f, (256, 128))
    nf = f // tn
    wlead = (None,) * len(lead)
    return pl.pallas_call(
        _swiglu_kernel,
        grid=(m // tm, nf),
        in_specs=[pl.BlockSpec((tm, d), lambda i, j: (i, 0)),
                  pl.BlockSpec(wlead + (d, tn), lambda i, j: lead + (0, j)),
                  pl.BlockSpec(wlead + (d, tn), lambda i, j: lead + (0, nf + j))],
        out_specs=pl.BlockSpec((tm, tn), lambda i, j: (i, j)),
        out_shape=jax.ShapeDtypeStruct((m, f), BF16),
        compiler_params=_cparams("parallel", "arbitrary"),
        name="swiglu",
    )(a, w_gu, w_gu)


MOE_ROW_TILE = 768
MOE_GATHER_ROWS = 256


def _moe_plan(route, n_experts, tm):
    m = route.shape[0]
    ids = route[:, :TOP_K].astype(jnp.int32).reshape(-1)
    onehot = (ids[:, None] == jnp.arange(n_experts, dtype=jnp.int32)[None, :]).astype(jnp.int32)
    csum = jnp.cumsum(onehot, axis=0)
    counts = csum[-1]
    rank = jnp.sum(onehot * csum, axis=1) - 1
    tiles_per = (counts + tm - 1) // tm
    tile_end = jnp.cumsum(tiles_per)
    row_start = (tile_end - tiles_per) * tm
    pos = jnp.sum(onehot * row_start[None, :], axis=1) + rank
    n_tiles = (TOP_K * m) // tm + n_experts
    row_token = jnp.zeros((n_tiles * tm,), jnp.int32).at[pos].set(jnp.arange(TOP_K * m, dtype=jnp.int32) // TOP_K)
    t = jnp.arange(n_tiles, dtype=jnp.int32)
    n_used = tile_end[-1]
    tile_expert = jnp.sum((t[:, None] >= tile_end[None, :]).astype(jnp.int32), axis=1)
    last_expert = jnp.sum((n_used - 1 >= tile_end).astype(jnp.int32))
    tile_expert = jnp.where(t < n_used, tile_expert, last_expert)
    return row_token, tile_expert, n_used.reshape(1), pos.reshape(m, TOP_K)


def _moe_gather_kernel(nu_ref, tok_ref, tok_next_ref, h_hbm, o_ref, buf, sem, *, tiles_per_row_tile):
    i = pl.program_id(0)
    tg = buf.shape[1]
    n_active = nu_ref[0] * tiles_per_row_tile
    slot = i % 2

    def copy(idx_ref, k, s):
        return pltpu.make_async_copy(h_hbm.at[pl.ds(idx_ref[0, k], 1), :], buf.at[s, pl.ds(k, 1), :], sem.at[s])

    def start_tile(idx_ref, s):
        def start(k2, carry):
            copy(idx_ref, 2 * k2, s).start(priority=0)
            copy(idx_ref, 2 * k2 + 1, s).start(priority=1)
            return carry

        lax.fori_loop(0, tg // 2, start, 0)

    @pl.when(i == 0)
    def _():
        start_tile(tok_ref, 0)

    @pl.when(i + 1 < n_active)
    def _():
        start_tile(tok_next_ref, 1 - slot)

    @pl.when(i < n_active)
    def _():
        def wait(k, carry):
            copy(tok_ref, k, slot).wait()
            return carry

        lax.fori_loop(0, tg, wait, 0)
        o_ref[...] = buf[slot].astype(o_ref.dtype)

    @pl.when(i >= n_active)
    def _():
        o_ref[...] = jnp.zeros_like(o_ref)


def _moe_gather(h, row_token, n_used, tm):
    m, d = h.shape
    rows = row_token.shape[0]
    tg = _pick(tm, (MOE_GATHER_ROWS, 128, 64, 32, 16, 8))
    nt = rows // tg
    tok = row_token.reshape(nt, 1, tg)
    return pl.pallas_call(
        functools.partial(_moe_gather_kernel, tiles_per_row_tile=tm // tg),
        grid_spec=pltpu.PrefetchScalarGridSpec(
            num_scalar_prefetch=1, grid=(nt,),
            in_specs=[pl.BlockSpec((None, 1, tg), lambda i, nu: (i, 0, 0), memory_space=pltpu.SMEM),
                      pl.BlockSpec((None, 1, tg), lambda i, nu: (jnp.minimum(i + 1, nt - 1), 0, 0),
                                   memory_space=pltpu.SMEM),
                      pl.BlockSpec(memory_space=pl.ANY)],
            out_specs=pl.BlockSpec((tg, d), lambda i, nu: (i, 0)),
            scratch_shapes=[pltpu.VMEM((2, tg, d), F32), pltpu.SemaphoreType.DMA((2,))]),
        out_shape=jax.ShapeDtypeStruct((rows, d), BF16),
        compiler_params=_cparams("arbitrary"),
        name="moe_gather",
    )(n_used, tok, tok, h)


def _moe_swiglu_kernel(te_ref, nu_ref, a_ref, wg_ref, wu_ref, o_ref):
    @pl.when(pl.program_id(0) < nu_ref[0])
    def _():
        a = a_ref[...]
        g = _dot(a, wg_ref[...].astype(BF16))
        u = _dot(a, wu_ref[...].astype(BF16))
        o_ref[...] = (_silu(g) * u).astype(o_ref.dtype)

    @pl.when(pl.program_id(0) >= nu_ref[0])
    def _():
        o_ref[...] = jnp.zeros_like(o_ref)


def _moe_swiglu(a, w_gu, fi, tile_expert, n_used, tm):
    rows, d = a.shape
    f = w_gu.shape[-1] // 2
    tn = _pick(f, (256, 128))
    nf = f // tn
    return pl.pallas_call(
        _moe_swiglu_kernel,
        grid_spec=pltpu.PrefetchScalarGridSpec(
            num_scalar_prefetch=2, grid=(rows // tm, nf),
            in_specs=[pl.BlockSpec((tm, d), lambda t, j, te, nu: (t, 0)),
                      pl.BlockSpec((None, None, d, tn), lambda t, j, te, nu: (fi, te[t], 0, j)),
                      pl.BlockSpec((None, None, d, tn), lambda t, j, te, nu: (fi, te[t], 0, nf + j))],
            out_specs=pl.BlockSpec((tm, tn), lambda t, j, te, nu: (t, j))),
        out_shape=jax.ShapeDtypeStruct((rows, f), BF16),
        compiler_params=_cparams("parallel", "arbitrary"),
        name="moe_swiglu",
    )(tile_expert, n_used, a, w_gu, w_gu)


def _moe_down_kernel(te_ref, nu_ref, a_ref, w_ref, o_ref):
    @pl.when(pl.program_id(0) < nu_ref[0])
    def _():
        o_ref[...] = _dot(a_ref[...], w_ref[...].astype(BF16))

    @pl.when(pl.program_id(0) >= nu_ref[0])
    def _():
        o_ref[...] = jnp.zeros_like(o_ref)


def _moe_down(a, w_down, fi, tile_expert, n_used, tm):
    rows, f = a.shape
    d = w_down.shape[-1]
    tn = _pick(d, (512, 256, 128))
    return pl.pallas_call(
        _moe_down_kernel,
        grid_spec=pltpu.PrefetchScalarGridSpec(
            num_scalar_prefetch=2, grid=(rows // tm, d // tn),
            in_specs=[pl.BlockSpec((tm, f), lambda t, j, te, nu: (t, 0)),
                      pl.BlockSpec((None, None, f, tn), lambda t, j, te, nu: (fi, te[t], 0, j))],
            out_specs=pl.BlockSpec((tm, tn), lambda t, j, te, nu: (t, j))),
        out_shape=jax.ShapeDtypeStruct((rows, d), F32),
        compiler_params=_cparams("parallel", "arbitrary"),
        name="moe_down",
    )(tile_expert, n_used, a, w_down)


def _resid_moe_kernel(meta_ref, pos_ref, pos_next_ref, x_ref, route_ref, g_ref, mod_ref, y_hbm, xo_ref, buf, sem):
    i = pl.program_id(0)
    tb = x_ref.shape[0]
    slot = i % 2

    def copy(idx_ref, k, e, s):
        return pltpu.make_async_copy(y_hbm.at[pl.ds(idx_ref[0, TOP_K * k + e], 1), :],
                                     buf.at[s, e, pl.ds(k, 1), :], sem.at[s])

    def start_tile(idx_ref, s):
        def start(k, carry):
            for e in range(TOP_K):
                copy(idx_ref, k, e, s).start(priority=e % 2)
            return carry

        lax.fori_loop(0, tb, start, 0)

    @pl.when(i == 0)
    def _():
        start_tile(pos_ref, 0)

    @pl.when(i + 1 < pl.num_programs(0))
    def _():
        start_tile(pos_next_ref, 1 - slot)

    def wait(k, carry):
        for e in range(TOP_K):
            copy(pos_ref, k, e, slot).wait()
        return carry

    lax.fori_loop(0, tb, wait, 0)
    route = route_ref[...]
    f = route[:, 2:3] * buf[slot, 0] + route[:, 3:4] * buf[slot, 1]
    xo_ref[...] = x_ref[...] + mod_ref[5:6, :] * _rms(f, g_ref[3:4, :])


def _resid_moe(lay, x, y_sorted, pos, route, g4, mods):
    m, d = x.shape
    tb = lay.tb
    row = lambda i, mt: (i, 0)
    pos3 = pos.reshape(lay.nblk, 1, TOP_K * tb)
    return pl.pallas_call(
        _resid_moe_kernel,
        grid_spec=pltpu.PrefetchScalarGridSpec(
            num_scalar_prefetch=1, grid=(lay.nblk,),
            in_specs=[pl.BlockSpec((None, 1, TOP_K * tb), lambda i, mt: (i, 0, 0), memory_space=pltpu.SMEM),
                      pl.BlockSpec((None, 1, TOP_K * tb), lambda i, mt: (jnp.minimum(i + 1, lay.nblk - 1), 0, 0),
                                   memory_space=pltpu.SMEM),
                      pl.BlockSpec((tb, d), row), pl.BlockSpec((tb, LANES), row),
                      pl.BlockSpec((4, d), lambda i, mt: (0, 0)),
                      pl.BlockSpec((None, 6, d), lambda i, mt: (mt[0, i], 0, 0)),
                      pl.BlockSpec(memory_space=pl.ANY)],
            out_specs=pl.BlockSpec((tb, d), row),
            scratch_shapes=[pltpu.VMEM((2, TOP_K, tb, d), F32), pltpu.SemaphoreType.DMA((2,))]),
        out_shape=jax.ShapeDtypeStruct((m, d), F32),
        compiler_params=_cparams("arbitrary"),
        name="resid_moe",
    )(lay.meta, pos3, pos3, x, route, g4, mods, y_sorted)


def _softmax_rows(s):
    e = jnp.exp(s - jnp.max(s, axis=-1, keepdims=True))
    return e * (1.0 / jnp.sum(e, axis=-1, keepdims=True))


def _attn_ctx_kernel(q_ref, k_ref, v_ref, _, o_ref, *, heads, scale):
    for h in range(heads):
        sl = slice(h * ATTN_HEAD_DIM, (h + 1) * ATTN_HEAD_DIM)
        s = _dot_nt(q_ref[:, sl].astype(BF16), k_ref[:, sl].astype(BF16)) * scale
        p = _softmax_rows(s).astype(BF16)
        o_ref[:, sl] = _dot(p, v_ref[:, sl].astype(BF16)).astype(o_ref.dtype)


def _attn_context(qkv, n_seq, seq_len, d, o_buf):
    n_heads = d // ATTN_HEAD_DIM
    hb = _pick(n_heads, (8, 4, 2, 1))
    wb = hb * ATTN_HEAD_DIM
    ncb = d // wb
    return pl.pallas_call(
        functools.partial(_attn_ctx_kernel, heads=hb, scale=ATTN_HEAD_DIM ** -0.5),
        grid=(n_seq, ncb),
        in_specs=[pl.BlockSpec((seq_len, wb), lambda b, h: (b, h)),
                  pl.BlockSpec((seq_len, wb), lambda b, h: (b, ncb + h)),
                  pl.BlockSpec((seq_len, wb), lambda b, h: (b, 2 * ncb + h)),
                  pl.BlockSpec(memory_space=pl.ANY)],
        out_specs=pl.BlockSpec((seq_len, wb), lambda b, h: (b, h)),
        out_shape=jax.ShapeDtypeStruct(o_buf.shape, o_buf.dtype),
        input_output_aliases={3: 0},
        compiler_params=_cparams("parallel", "parallel"),
        name="attn_context",
    )(qkv, qkv, qkv, o_buf)


def _attn_lat_kernel(q_ref, k_ref, v_ref, ck_ref, cv_ref, tb_ref, _, o_ref, *, rows, wr, scale):
    kb = k_ref[...].astype(BF16)
    vb = v_ref[...].astype(BF16)
    ckb = ck_ref[...].astype(BF16)
    cvb = cv_ref[...].astype(BF16)
    group = ATTN_LAT_ROW_GROUP if rows % ATTN_LAT_ROW_GROUP == 0 else 1
    row_start = [min(max(r - wr // 2, 0), rows - wr) for r in range(rows)]
    neg = jnp.full((GRID_W, GRID_W), NEG_INF, F32)
    biases = {}
    for r0 in range(0, rows, group):
        qrows = range(r0, r0 + group)
        k0 = row_start[r0]
        k1 = row_start[r0 + group - 1] + wr
        pattern = tuple(tuple(kr - r + MAX_WIN_ROWS - 1 if row_start[r] <= kr < row_start[r] + wr else None
                              for kr in range(k0, k1)) for r in qrows)
        if pattern not in biases:
            biases[pattern] = jnp.concatenate(
                [jnp.concatenate([neg if dr is None else tb_ref[dr] for dr in prow], axis=1) for prow in pattern],
                axis=0)
        q = q_ref[r0 * GRID_W:(r0 + group) * GRID_W, :].astype(BF16)
        kl = kb[k0 * GRID_W:k1 * GRID_W]
        vl = vb[k0 * GRID_W:k1 * GRID_W]
        s_loc = _dot_nt(q, kl) * scale + biases[pattern]
        s_ctx = _dot_nt(q, ckb) * scale
        mx = jnp.maximum(jnp.max(s_loc, axis=-1, keepdims=True), jnp.max(s_ctx, axis=-1, keepdims=True))
        e_loc = jnp.exp(s_loc - mx)
        e_ctx = jnp.exp(s_ctx - mx)
        inv = 1.0 / (jnp.sum(e_loc, axis=-1, keepdims=True) + jnp.sum(e_ctx, axis=-1, keepdims=True))
        o = (_dot(e_loc.astype(BF16), vl) + _dot(e_ctx.astype(BF16), cvb)) * inv
        o_ref[r0 * GRID_W:(r0 + group) * GRID_W, :] = o.astype(o_ref.dtype)


def _rel_bias_table(rpb):
    qc = np.arange(GRID_W)[:, None]
    kc = np.arange(GRID_W)[None, :]
    ws = np.clip(qc - WIN_COLS // 2, 0, GRID_W - WIN_COLS)
    in_win = (kc >= ws) & (kc < ws + WIN_COLS)
    dcol = np.clip(kc - qc, 1 - WIN_COLS, WIN_COLS - 1) + WIN_COLS - 1
    return jnp.where(in_win[None, None], rpb.astype(F32)[:, :, dcol], NEG_INF)


def _attn_latent(qkv, cache_k, cache_v, mi, rpb, row_block0, n_seq, seq_len, d, o_buf):
    n_heads = d // ATTN_HEAD_DIM
    rows = seq_len // GRID_W
    wr = min(MAX_WIN_ROWS, rows)
    past = cache_k.shape[2]
    ck = cache_k.reshape(cache_k.shape[0], cache_k.shape[1], past, d)
    cv = cache_v.reshape(cache_v.shape[0], cache_v.shape[1], past, d)
    table = _rel_bias_table(rpb)
    hd = ATTN_HEAD_DIM
    return pl.pallas_call(
        functools.partial(_attn_lat_kernel, rows=rows, wr=wr, scale=ATTN_HEAD_DIM ** -0.5),
        grid=(n_seq, n_heads),
        in_specs=[pl.BlockSpec((seq_len, hd), lambda b, h: (row_block0 + b, h)),
                  pl.BlockSpec((seq_len, hd), lambda b, h: (row_block0 + b, n_heads + h)),
                  pl.BlockSpec((seq_len, hd), lambda b, h: (row_block0 + b, 2 * n_heads + h)),
                  pl.BlockSpec((None, None, past, hd), lambda b, h: (b, mi, 0, h)),
                  pl.BlockSpec((None, None, past, hd), lambda b, h: (b, mi, 0, h)),
                  pl.BlockSpec((None, 2 * MAX_WIN_ROWS - 1, GRID_W, GRID_W), lambda b, h: (h, 0, 0, 0)),
                  pl.BlockSpec(memory_space=pl.ANY)],
        out_specs=pl.BlockSpec((seq_len, hd), lambda b, h: (row_block0 + b, h)),
        out_shape=jax.ShapeDtypeStruct(o_buf.shape, o_buf.dtype),
        input_output_aliases={6: 0},
        compiler_params=_cparams("parallel", "parallel"),
        name="attn_latent",
    )(qkv, qkv, qkv, ck, cv, table, o_buf)


def _halo_rows(meta_ref, h_prev_blk, h_next_blk):
    i = pl.program_id(0)
    hp = jnp.where(meta_ref[1, i] == 1, 0.0, h_prev_blk[POOL_HALO - 1:POOL_HALO, :])
    hn = jnp.where(meta_ref[2, i] == 1, 0.0, h_next_blk[0:1, :])
    return hp, hn


def _rwkv_prep_kernel(meta_ref, x_ref, xp_ref, xn_ref, g_ref, mod_ref, mu_ref, *o_refs):
    g = g_ref[0:1, :]
    h = _modulated(x_ref[...], g, mod_ref, 0, 1)
    hp, hn = _halo_rows(meta_ref, _modulated(xp_ref[...], g, mod_ref, 0, 1),
                        _modulated(xn_ref[...], g, mod_ref, 0, 1))
    tb = h.shape[0]
    row = lax.broadcasted_iota(jnp.int32, h.shape, 0)
    prev = jnp.where(row == 0, hp, pltpu.roll(h, 1, 0))
    nxt = jnp.where(row == tb - 1, hn, pltpu.roll(h, tb - 1, 0))
    xx = 0.5 * (prev + nxt) - h
    for n, o_ref in enumerate(o_refs):
        o_ref[...] = (h + xx * mu_ref[n:n + 1, :]).astype(o_ref.dtype)


def _halo_specs(lay, d):
    tb = lay.tb
    per = tb // POOL_HALO
    last = lay.nblk * per - 1
    return [pl.BlockSpec((tb, d), lambda i, mt: (i, 0)),
            pl.BlockSpec((POOL_HALO, d), lambda i, mt: (jnp.maximum(i * per - 1, 0), 0)),
            pl.BlockSpec((POOL_HALO, d), lambda i, mt: (jnp.minimum((i + 1) * per, last), 0))]


def _rwkv_prep(lay, x, g4, mods, mu):
    m, d = x.shape
    tb = lay.tb
    return pl.pallas_call(
        _rwkv_prep_kernel,
        grid_spec=pltpu.PrefetchScalarGridSpec(
            num_scalar_prefetch=1, grid=(lay.nblk,),
            in_specs=_halo_specs(lay, d) + [
                pl.BlockSpec((4, d), lambda i, mt: (0, 0)),
                pl.BlockSpec((None, 6, d), lambda i, mt: (mt[0, i], 0, 0)),
                pl.BlockSpec((6, d), lambda i, mt: (0, 0))],
            out_specs=[pl.BlockSpec((tb, d), lambda i, mt: (i, 0))] * 6),
        out_shape=[jax.ShapeDtypeStruct((m, d), BF16)] * 6,
        compiler_params=_cparams("parallel"),
        name="rwkv_prep",
    )(lay.meta, x, x, x, g4, mods, mu)


def _split(x):
    hi = x.astype(BF16)
    return hi, (x - hi.astype(F32)).astype(BF16)


def _dot3(a, b):
    (ah, al), (bh, bl) = a, b
    return _dot(jnp.concatenate([ah, ah, al], axis=1), jnp.concatenate([bh, bl, bh], axis=0))


def _dot3_nt(a, b):
    (ah, al), (bh, bl) = a, b
    return _dot_nt(jnp.concatenate([ah, ah, al], axis=1), jnp.concatenate([bh, bl, bh], axis=1))


def _rwkv_scan_kernel(*refs, seq_len, has_s0, pairs):
    if has_s0:
        (r_ref, k_ref, v_ref, w0_ref, w1_ref, a0_ref, a1_ref, g_ref, pv_ref, s0_ref, _, y_ref, yf_ref, yb_ref) = refs
        sf_ref = None
    else:
        (r_ref, k_ref, v_ref, w0_ref, w1_ref, a0_ref, a1_ref, g_ref, pv_ref, _, y_ref, sf_ref, yf_ref, yb_ref) = refs
        s0_ref = None
    c = RWKV_CHUNK
    n = RWKV_HEAD_DIM
    nchunks = seq_len // c
    wpre_refs = (w0_ref, w1_ref)
    apre_refs = (a0_ref, a1_ref)
    def iota(shape, dim):
        return lax.broadcasted_iota(jnp.int32, shape, dim)

    head0 = iota((c, LANES), 1) < n
    seg = (iota((LANES, LANES), 0) // n == iota((LANES, LANES), 1) // n)
    seg_f = seg.astype(F32)
    seg_b = seg_f.astype(BF16)
    tt = iota((c, 2 * c), 0)
    ss = iota((c, 2 * c), 1) % c
    lo_half = iota((c, 2 * c), 1) < c
    strict = (ss < tt, ss > tt)
    incl = (ss <= tt, ss >= tt)
    t2 = iota((c, c), 0)
    s2 = iota((c, c), 1)
    tri = ((s2 <= t2).astype(F32).astype(BF16), (s2 >= t2).astype(F32).astype(BF16))

    def seg_sum(x):
        hi, lo = _split(x)
        return _dot(jnp.concatenate([hi, lo], axis=1), jnp.concatenate([seg_b, seg_b], axis=0))

    head0_b = head0.astype(F32).astype(BF16)
    head1_b = (1.0 - head0.astype(F32)).astype(BF16)
    lo_half_b = lo_half.astype(F32).astype(BF16)
    hi_half_b = (1.0 - lo_half.astype(F32)).astype(BF16)

    def stack2(x):
        return jnp.concatenate([x * head0_b, x * head1_b], axis=0)

    def stack2_parts(parts):
        return tuple(stack2(p) for p in parts)

    def blockdiag(x):
        return jnp.concatenate([x * lo_half_b, x * hi_half_b], axis=0)

    def cat0(a, b):
        return tuple(jnp.concatenate([x, y], axis=0) for x, y in zip(a, b))

    def features(rows, lanes, d):
        k = k_ref[rows, lanes]
        kk = k * pv_ref[0:1, lanes]
        kk = kk / jnp.maximum(jnp.sqrt(seg_sum(kk * kk)), 1e-12)
        wx = -(pv_ref[5 + d:6 + d, lanes] + wpre_refs[d][rows, lanes])
        w_log = -(jnp.maximum(wx, 0.0) + jnp.log(1.0 + jnp.exp(-jnp.abs(wx)))) - 0.5
        logw = -jnp.exp(w_log)
        a = jax.nn.sigmoid(pv_ref[7 + d:8 + d, lanes] + apre_refs[d][rows, lanes])
        kd = k * (1.0 + (a - 1.0) * pv_ref[1:2, lanes])
        return kk, kd, kk * a, logw

    def chunk(rows, lanes, d, s_bd):
        r = r_ref[rows, lanes]
        v = v_ref[rows, lanes]
        kk, kd, bb, logw = features(rows, lanes, d)
        yield
        l1 = logw.astype(BF16)
        l2 = (logw - l1.astype(F32)).astype(BF16)
        l3 = (logw - l1.astype(F32) - l2.astype(F32)).astype(BF16)
        cum = _dot(jnp.concatenate([tri[d]] * 3, axis=1), jnp.concatenate([l1, l2, l3], axis=0))
        yield
        total = cum[c - 1:c, :] if d == 0 else cum[0:1, :]
        e_inv = jnp.exp(-cum)
        e_rest = jnp.exp(total - cum)
        at = _split(kk * jnp.exp(cum - logw))
        rt = (r * jnp.exp(cum)).astype(BF16)
        kb = cat0(stack2_parts(_split(kd * e_inv)), stack2_parts(_split(bb * e_inv)))
        s_parts = _split(s_bd)
        kbs = cat0(kb, s_parts)
        sa = _dot3_nt(at, kbs)
        sr = _dot_nt(rt, kbs[0])
        yield
        zero = jnp.zeros((c, 2 * c), F32)
        m_cat = jnp.where(strict[d], sa[:, :2 * c], zero)
        l_cat = jnp.where(strict[d], sa[:, 2 * c:4 * c], zero)
        rk_cat = jnp.where(incl[d], sr[:, :2 * c], zero)
        rb_cat = jnp.where(incl[d], sr[:, 2 * c:4 * c], zero)
        y0 = sr[:, 4 * c:]
        v_st = stack2_parts(_split(v))
        x = sa[:, 4 * c:] + _dot3(_split(m_cat), v_st)
        yield
        p = -l_cat
        n_stage = c.bit_length() - 1
        for stage in range(n_stage):
            pp = _split(p)
            xs = stack2_parts(_split(x))
            if stage + 1 < n_stage:
                px = _dot3(pp, tuple(jnp.concatenate([xi, blockdiag(pi)], axis=1) for xi, pi in zip(xs, pp)))
                x = x + px[:, :LANES]
                p = px[:, LANES:]
            else:
                x = x + _dot3(pp, xs)
            yield
        u = x
        u_parts = _split(u)
        y = y0 + _dot(
            jnp.concatenate([rk_cat, rb_cat], axis=1).astype(BF16),
            jnp.concatenate([v_st[0], -stack2(u_parts[0])], axis=0))
        vu_t = _split(jnp.concatenate([v, -u], axis=0).T)
        kb_rest = cat0(_split(kd * e_rest), _split(bb * e_rest))
        s_new = s_bd * jnp.exp(total) + seg_f * _dot3(vu_t, kb_rest)
        return y, s_new

    def run_lockstep(gens):
        results = [None] * len(gens)
        active = list(range(len(gens)))
        while active:
            for i in list(active):
                try:
                    next(gens[i])
                except StopIteration as stop:
                    results[i] = stop.value
                    active.remove(i)
        return results

    chains = [(pi, d) for pi in range(pairs) for d in range(2)]
    if has_s0:
        z = jnp.zeros((n, n), F32)
        s_init = tuple(
            jnp.concatenate([jnp.concatenate([s0_ref[d, 2 * pi], z], axis=1),
                             jnp.concatenate([z, s0_ref[d, 2 * pi + 1]], axis=1)], axis=0) for pi, d in chains)
    else:
        s_init = tuple(jnp.zeros((LANES, LANES), F32) for _ in chains)

    def body(ci, carry):
        rows_fb = (pl.ds(pl.multiple_of(ci * c, c), c), pl.ds(pl.multiple_of((nchunks - 1 - ci) * c, c), c))
        lanes = [slice(pi * LANES, (pi + 1) * LANES) for pi, _ in chains]
        results = run_lockstep([chunk(rows_fb[d], ln, d, s_bd)
                                for (_, d), ln, s_bd in zip(chains, lanes, carry)])
        for (_, d), ln, (y, _) in zip(chains, lanes, results):
            (yf_ref, yb_ref)[d][rows_fb[d], ln] = y
        return tuple(s_bd for _, s_bd in results)

    s_fin = lax.fori_loop(0, nchunks, body, s_init)
    if sf_ref is not None:
        for (pi, d), s_bd in zip(chains, s_fin):
            sf_ref[d, 2 * pi] = s_bd[:n, :n]
            sf_ref[d, 2 * pi + 1] = s_bd[n:, n:]

    fr = 2 * c if seq_len % (2 * c) == 0 else c

    def finish_rows(rows, lanes):
        y = yf_ref[rows, lanes] + yb_ref[rows, lanes]
        a_sum = sum(jax.nn.sigmoid(pv_ref[7 + d:8 + d, lanes] + apre_refs[d][rows, lanes]) for d in range(2))
        kd_sum = k_ref[rows, lanes] * (2.0 + (a_sum - 2.0) * pv_ref[1:2, lanes])
        sums = seg_sum(jnp.concatenate([y, r_ref[rows, lanes] * kd_sum * pv_ref[2:3, lanes]], axis=0))
        yield
        yc = y - sums[:fr] * (1.0 / n)
        var = seg_sum(yc * yc) * (1.0 / n)
        yield
        y = yc * lax.rsqrt(var + LNX_EPS) * pv_ref[3:4, lanes] + pv_ref[4:5, lanes] + sums[fr:] * v_ref[rows, lanes]
        y_ref[rows, lanes] = (y * g_ref[rows, lanes]).astype(y_ref.dtype)

    def finish(ci, carry):
        rows = pl.ds(pl.multiple_of(ci * fr, fr), fr)
        run_lockstep([finish_rows(rows, slice(pi * LANES, (pi + 1) * LANES)) for pi in range(pairs)])
        return carry

    lax.fori_loop(0, seq_len // fr, finish, 0)


def _rwkv_scan(r, k, v, wpre, apre, g, pvec, *, row_block0, n_seq, seq_len, s0=None, mi=0, y_buf=None):
    m, d = r.shape
    n = RWKV_HEAD_DIM
    def fits(p, bufs):
        return (d // LANES) % p == 0 and (bufs * 8 + 2 + 1) * seq_len * p * LANES * 4 <= RWKV_VMEM_BUDGET_BYTES

    pairs, bufs = next((p, b) for p in (4, 2, 1) for b in (2, 1) if fits(p, b))
    wb = pairs * LANES
    nblk = d // wb
    seq_map = lambda b, p: (row_block0 + b, p)
    blk = pl.BlockSpec((seq_len, wb), seq_map)
    blk_in = blk if bufs == 2 else pl.BlockSpec((seq_len, wb), seq_map, pipeline_mode=pl.Buffered(1))
    in_specs = [blk_in] * 8 + [pl.BlockSpec((16, wb), lambda b, p: (0, p))]
    args = [r, k, v, wpre[0], wpre[1], apre[0], apre[1], g, pvec]
    y_shape = jax.ShapeDtypeStruct((m, d), BF16)
    if s0 is not None:
        in_specs.append(pl.BlockSpec((None, None, 2, 2 * pairs, n, n), lambda b, p: (b, mi, 0, p, 0, 0)))
        args.append(s0)
    in_specs.append(pl.BlockSpec(memory_space=pl.ANY))
    args.append(y_buf)
    aliases = {len(args) - 1: 0}
    if s0 is not None:
        out_specs, out_shape = blk, y_shape
    else:
        out_specs = [blk, pl.BlockSpec((None, 2, 2 * pairs, n, n), lambda b, p: (b, 0, p, 0, 0))]
        out_shape = [y_shape, jax.ShapeDtypeStruct((n_seq, 2, d // n, n, n), F32)]
    return pl.pallas_call(
        functools.partial(_rwkv_scan_kernel, seq_len=seq_len, has_s0=s0 is not None, pairs=pairs),
        grid=(n_seq, nblk),
        in_specs=in_specs, out_specs=out_specs, out_shape=out_shape, input_output_aliases=aliases,
        scratch_shapes=[pltpu.VMEM((seq_len, wb), F32), pltpu.VMEM((seq_len, wb), F32)],
        compiler_params=_cparams("parallel", "parallel"),
        name="rwkv_scan",
    )(*args)


def _pool_prep_kernel(meta_ref, x_ref, xp_ref, xn_ref, g_ref, mod_ref, o_ref):
    i = pl.program_id(0)
    g = g_ref[0:1, :]
    h = _modulated(x_ref[...], g, mod_ref, 0, 1)
    hp = jnp.where(meta_ref[1, i] == 1, 0.0, _modulated(xp_ref[...], g, mod_ref, 0, 1))
    hn = jnp.where(meta_ref[2, i] == 1, 0.0, _modulated(xn_ref[...], g, mod_ref, 0, 1))
    tb, d = h.shape
    pd = d // len(POOL_WINDOWS)
    ext = jnp.concatenate([hp, h, hn], axis=0)
    ne = tb + 2 * POOL_HALO
    row = lax.broadcasted_iota(jnp.int32, (tb, pd), 0)
    at_first = (meta_ref[1, i] == 1).astype(jnp.int32)
    at_last = (meta_ref[2, i] == 1).astype(jnp.int32)
    acc = ext[:, 0:d] + pltpu.roll(ext, 1, 0)
    half = 1
    for gi, win in enumerate(POOL_WINDOWS):
        if gi > 0:
            sub = acc[:, pd:]
            acc = pltpu.roll(sub, half, 0) + pltpu.roll(sub, ne - half, 0)
            half *= 2
        wsum = acc[POOL_HALO:POOL_HALO + tb, 0:pd]
        missing = (at_first * jnp.maximum(win // 2 - row, 0) + at_last * jnp.maximum(row + win // 2 - tb, 0))
        cnt = (win - missing).astype(F32)
        o_ref[:, gi * pd:(gi + 1) * pd] = (wsum / cnt - h[:, gi * pd:(gi + 1) * pd]).astype(o_ref.dtype)


def _pool_prep(lay, x, g4, mods):
    m, d = x.shape
    tb = lay.tb
    return pl.pallas_call(
        _pool_prep_kernel,
        grid_spec=pltpu.PrefetchScalarGridSpec(
            num_scalar_prefetch=1, grid=(lay.nblk,),
            in_specs=_halo_specs(lay, d) + [
                pl.BlockSpec((4, d), lambda i, mt: (0, 0)),
                pl.BlockSpec((None, 6, d), lambda i, mt: (mt[0, i], 0, 0))],
            out_specs=pl.BlockSpec((tb, d), lambda i, mt: (i, 0))),
        out_shape=jax.ShapeDtypeStruct((m, d), BF16),
        compiler_params=_cparams("parallel"),
        name="pool_prep",
    )(lay.meta, x, x, x, g4, mods)


def kernel(x_prompt, x_sample, cache_k, cache_v, state_rwkv, c, c_ctx, ada_w, ada_b, norm_g,
           attn_w_qkv, attn_w_o, attn_rpb, rwkv_mu, rwkv_w_r, rwkv_w_k, rwkv_w_v, rwkv_w_o,
           rwkv_w0, rwkv_w1, rwkv_w2, rwkv_a0, rwkv_a1, rwkv_a2, rwkv_g1, rwkv_g2, rwkv_k_k, rwkv_k_a,
           rwkv_r_k, rwkv_lnx_w, rwkv_lnx_b, pool_w, pool_scale, ffn_w_gu, ffn_w_down,
           moe_router, moe_w_gu, moe_w_down):
    nb, seq, d = x_prompt.shape
    ndb, dseq, _ = x_sample.shape
    depth = ada_w.shape[0]
    n_experts = moe_router.shape[-1]
    lay = _Layout(nb, seq, ndb, dseq, _pick(np.gcd(seq, dseq), (TOK_BLOCK, 128, 64, 32, 16)))
    lay_prep = _Layout(nb, seq, ndb, dseq, _pick(np.gcd(seq, dseq), (TOK_BLOCK // 2, 64, 32, 16)))
    assert lay.mp % dseq == 0
    lat_block0 = lay.mp // dseq

    x = jnp.concatenate([x_prompt.reshape(lay.mp, d), x_sample.reshape(lay.ms, d)], axis=0)
    cond8 = jnp.zeros((8, d), F32).at[0].set(c_ctx).at[1:1 + ndb].set(c)
    mods_all = _ada_all(cond8, ada_w, ada_b).reshape(depth, 8, 6, d)

    new_k, new_v, new_s = [], [], []
    for i in range(depth):
        kind, mi, fi = i % 3, i // 3, i // 2
        mods = mods_all[i]
        g4 = norm_g[i]
        if kind == 0:
            h = _norm_mod(lay, x, g4, mods, g_row=0, shift_row=0, scale_row=1)
            qkv = _mm(h, attn_w_qkv, (mi,))
            o = _attn_context(qkv, nb, seq, d, jnp.zeros((lay.m, d), BF16))
            o = _attn_latent(qkv, cache_k, cache_v, mi, attn_rpb[mi], lat_block0, ndb, dseq, d, o)
            mix = _mm(o, attn_w_o, (mi,))
            heads = d // ATTN_HEAD_DIM
            new_k.append(qkv[:lay.mp, d:2 * d].reshape(nb, seq, heads, ATTN_HEAD_DIM))
            new_v.append(qkv[:lay.mp, 2 * d:].reshape(nb, seq, heads, ATTN_HEAD_DIM))
        elif kind == 1:
            xr, xw, xk, xv, xa, xg = _rwkv_prep(lay_prep, x, g4, mods, rwkv_mu[mi])
            r = _mm(xr, rwkv_w_r, (mi,))
            k = _mm(xk, rwkv_w_k, (mi,))
            v = _mm(xv, rwkv_w_v, (mi,))
            lw, la = rwkv_w1.shape[-1], rwkv_a1.shape[-1]
            w1cat = jnp.moveaxis(rwkv_w1[mi], 0, 1).reshape(d, 2 * lw)
            a1cat = jnp.moveaxis(rwkv_a1[mi], 0, 1).reshape(d, 2 * la)
            lg = rwkv_g1.shape[-1]
            lgp = -(-lg // LANES) * LANES
            g1p = jnp.pad(rwkv_g1[mi], ((0, 0), (0, lgp - lg)))
            g2p = jnp.pad(rwkv_g2[mi], ((0, lgp - lg), (0, 0)))
            tw = _mm(xw, w1cat)
            ta = _mm(xa, a1cat)
            tg = _mm(xg, g1p)
            wpre = [_mm(tw, rwkv_w2, (mi, dd), act="tanh", a_col=dd) for dd in range(2)]
            apre = [_mm(ta, rwkv_a2, (mi, dd), a_col=dd) for dd in range(2)]
            gate = _mm(tg, g2p, act="sigmoid")
            pvec = jnp.zeros((16, d), F32)
            for row, val in enumerate((rwkv_k_k[mi], rwkv_k_a[mi], rwkv_r_k[mi].reshape(d), rwkv_lnx_w[mi],
                                       rwkv_lnx_b[mi], rwkv_w0[mi, 0], rwkv_w0[mi, 1], rwkv_a0[mi, 0],
                                       rwkv_a0[mi, 1])):
                pvec = pvec.at[row].set(val)
            y, s_p = _rwkv_scan(r, k, v, wpre, apre, gate, pvec, row_block0=0, n_seq=nb, seq_len=seq,
                                y_buf=jnp.zeros((lay.m, d), BF16))
            y = _rwkv_scan(r, k, v, wpre, apre, gate, pvec, row_block0=lat_block0, n_seq=ndb, seq_len=dseq,
                           s0=state_rwkv, mi=mi, y_buf=y)
            mix = _mm(y, rwkv_w_o, (mi,))
            new_s.append(s_p)
        else:
            hd = _pool_prep(lay_prep, x, g4, mods)
            mix = _mm(hd, pool_w, (mi,), n_groups=len(POOL_WINDOWS), col_scale=pool_scale[mi].reshape(1, d))
        if i % 2 == 0:
            x, h = _resid_norm(lay, x, mix, g4, mods)
            hid = _swiglu_hidden(h, ffn_w_gu, (fi,))
            f = _mm(hid, ffn_w_down, (fi,), tn=_pick(d, (256, 128)), a_single_buffer=True)
        else:
            router = jnp.pad(moe_router[fi], ((0, 0), (0, LANES - n_experts)))
            x, h, route = _resid_norm(lay, x, mix, g4, mods, router=router, n_experts=n_experts)
            tm = _pick(TOP_K * lay.m, (MOE_ROW_TILE, 256, 128, 64, 32, 16, 8))
            row_token, tile_expert, n_used, pos = _moe_plan(route, n_experts, tm)
            xs = _moe_gather(h, row_token, n_used, tm)
            hid = _moe_swiglu(xs, moe_w_gu, fi, tile_expert, n_used, tm)
            ys = _moe_down(hid, moe_w_down, fi, tile_expert, n_used, tm)
            x = _resid_moe(lay, x, ys, pos, route, g4, mods)
            continue
        x = _resid(lay, x, f, g4, mods)

    y_prompt = x[:lay.mp].reshape(nb, seq, d)
    y_sample = x[lay.mp:].reshape(ndb, dseq, d)
    return (y_prompt, y_sample, jnp.stack(new_k, axis=1), jnp.stack(new_v, axis=1), jnp.stack(new_s, axis=1))
```

```python
import functools

import numpy as np
import jax
import jax.numpy as jnp
from jax import lax
from jax.experimental import pallas as pl
from jax.experimental.pallas import tpu as pltpu

F32 = jnp.float32
BF16 = jnp.bfloat16
HIGHEST = lax.Precision.HIGHEST

V7X_VMEM_LIMIT_BYTES = 56 * 1024 * 1024
RWKV_VMEM_BUDGET_BYTES = 46 * 1024 * 1024
LANES = 128

NORM_EPS = 1e-6
NEG_INF = -1e30
ATTN_HEAD_DIM = 128
GRID_W = 64
MAX_WIN_ROWS = 8
WIN_COLS = 16
ATTN_LAT_ROW_GROUP = 4
RWKV_HEAD_DIM = 64
RWKV_CHUNK = 64
LNX_EPS = 64e-5
POOL_WINDOWS = (2, 4, 8, 16)
POOL_HALO = 8
TOP_K = 2
TOK_BLOCK = 256


def _cparams(*sem):
    return pltpu.CompilerParams(dimension_semantics=sem, vmem_limit_bytes=V7X_VMEM_LIMIT_BYTES)


def _pick(n, candidates):
    for c in candidates:
        if c <= n and n % c == 0:
            return c
    return n


def _dot(a, b, precision=None):
    return jnp.dot(a, b, preferred_element_type=F32, precision=precision)


def _dot_nt(a, b, precision=None):
    return lax.dot_general(a, b, (((1,), (1,)), ((), ())), preferred_element_type=F32, precision=precision)


def _rms(x, g):
    return x * lax.rsqrt(jnp.mean(x * x, axis=-1, keepdims=True) + NORM_EPS) * g


def _silu(x):
    return x * jax.nn.sigmoid(x)


class _Layout:
    def __init__(self, n_prompt_seq, prompt_len, n_latent_seq, latent_len, tb):
        assert prompt_len % tb == 0 and latent_len % tb == 0
        self.tb = tb
        self.mp = n_prompt_seq * prompt_len
        self.ms = n_latent_seq * latent_len
        self.m = self.mp + self.ms
        self.prompt_len, self.latent_len = prompt_len, latent_len
        self.n_prompt_seq, self.n_latent_seq = n_prompt_seq, n_latent_seq
        rid, first, last = [], [], []
        for i in range(self.m // tb):
            row = i * tb
            if row < self.mp:
                rid.append(0)
                first.append(int(row % prompt_len == 0))
                last.append(int((row + tb) % prompt_len == 0))
            else:
                rid.append(1 + (row - self.mp) // latent_len)
                first.append(int((row - self.mp) % latent_len == 0))
                last.append(int((row - self.mp + tb) % latent_len == 0))
        self.meta = jnp.asarray(np.array([rid, first, last], np.int32))
        self.nblk = self.m // tb


def _ada_kernel(c_ref, w_ref, b_ref, o_ref):
    s = _silu(c_ref[...]).astype(BF16)
    o_ref[...] = _dot(s, w_ref[...].astype(BF16)) + b_ref[...]


def _ada_all(cond8, ada_w, ada_b):
    depth, d, n = ada_w.shape
    tn = _pick(n, (1024, 512, 256, 128))
    return pl.pallas_call(
        _ada_kernel,
        grid=(depth, n // tn),
        in_specs=[pl.BlockSpec((8, d), lambda l, j: (0, 0)),
                  pl.BlockSpec((None, d, tn), lambda l, j: (l, 0, j)),
                  pl.BlockSpec((None, 1, tn), lambda l, j: (l, 0, j))],
        out_specs=pl.BlockSpec((None, 8, tn), lambda l, j: (l, 0, j)),
        out_shape=jax.ShapeDtypeStruct((depth, 8, n), F32),
        compiler_params=_cparams("parallel", "parallel"),
        name="ada",
    )(cond8, ada_w, ada_b.reshape(depth, 1, n))


def _modulated(x, g, mod_ref, shift_row, scale_row):
    return _rms(x, g) * (1.0 + mod_ref[scale_row:scale_row + 1, :]) + mod_ref[shift_row:shift_row + 1, :]


def _router_route(h, rw_ref, n_experts):
    logits = _dot(h, rw_ref[...], precision=HIGHEST)
    lane = lax.broadcasted_iota(jnp.int32, logits.shape, 1)
    valid = lane < n_experts
    logits = jnp.where(valid, logits, NEG_INF)
    e = jnp.exp(logits - jnp.max(logits, axis=-1, keepdims=True))
    p = e / jnp.sum(e, axis=-1, keepdims=True)
    p = jnp.where(valid, p, -2.0)
    m1 = jnp.max(p, axis=-1, keepdims=True)
    i1 = jnp.min(jnp.where(p == m1, lane, LANES), axis=-1, keepdims=True)
    p2 = jnp.where(lane == i1, -1.0, p)
    m2 = jnp.max(p2, axis=-1, keepdims=True)
    i2 = jnp.min(jnp.where(p2 == m2, lane, LANES), axis=-1, keepdims=True)
    den = m1 + m2
    return (jnp.where(lane == 0, i1.astype(F32), 0.0) + jnp.where(lane == 1, i2.astype(F32), 0.0)
            + jnp.where(lane == 2, m1 / den, 0.0) + jnp.where(lane == 3, m2 / den, 0.0))


def _norm_mod_kernel(meta_ref, x_ref, g_ref, mod_ref, o_ref, *, g_row, shift_row, scale_row):
    h = _modulated(x_ref[...], g_ref[g_row:g_row + 1, :], mod_ref, shift_row, scale_row)
    o_ref[...] = h.astype(o_ref.dtype)


def _norm_mod(lay, x, g4, mods, *, g_row, shift_row, scale_row):
    m, d = x.shape
    tb = lay.tb
    return pl.pallas_call(
        functools.partial(_norm_mod_kernel, g_row=g_row, shift_row=shift_row, scale_row=scale_row),
        grid_spec=pltpu.PrefetchScalarGridSpec(
            num_scalar_prefetch=1, grid=(lay.nblk,),
            in_specs=[pl.BlockSpec((tb, d), lambda i, mt: (i, 0)),
                      pl.BlockSpec((4, d), lambda i, mt: (0, 0)),
                      pl.BlockSpec((None, 6, d), lambda i, mt: (mt[0, i], 0, 0))],
            out_specs=pl.BlockSpec((tb, d), lambda i, mt: (i, 0))),
        out_shape=jax.ShapeDtypeStruct((m, d), BF16),
        compiler_params=_cparams("parallel"),
        name="norm_mod",
    )(lay.meta, x, g4, mods)


def _resid_norm_kernel(meta_ref, x_ref, mix_ref, g_ref, mod_ref, *rest, n_experts):
    if n_experts:
        rw_ref, xo_ref, h_ref, route_ref = rest
    else:
        xo_ref, h_ref = rest
    x = x_ref[...] + mod_ref[2:3, :] * _rms(mix_ref[...], g_ref[1:2, :])
    xo_ref[...] = x
    h = _modulated(x, g_ref[2:3, :], mod_ref, 3, 4)
    h_ref[...] = h.astype(h_ref.dtype)
    if n_experts:
        route_ref[...] = _router_route(h, rw_ref, n_experts)


def _resid_norm(lay, x, mix, g4, mods, router=None, n_experts=0):
    m, d = x.shape
    tb = lay.tb
    row = lambda i, mt: (i, 0)
    in_specs = [pl.BlockSpec((tb, d), row), pl.BlockSpec((tb, d), row),
                pl.BlockSpec((4, d), lambda i, mt: (0, 0)),
                pl.BlockSpec((None, 6, d), lambda i, mt: (mt[0, i], 0, 0))]
    out_specs = [pl.BlockSpec((tb, d), row), pl.BlockSpec((tb, d), row)]
    out_shape = [jax.ShapeDtypeStruct((m, d), F32), jax.ShapeDtypeStruct((m, d), F32 if n_experts else BF16)]
    args = [lay.meta, x, mix, g4, mods]
    if n_experts:
        in_specs.append(pl.BlockSpec((d, LANES), lambda i, mt: (0, 0)))
        out_specs.append(pl.BlockSpec((tb, LANES), row))
        out_shape.append(jax.ShapeDtypeStruct((m, LANES), F32))
        args.append(router)
    return pl.pallas_call(
        functools.partial(_resid_norm_kernel, n_experts=n_experts),
        grid_spec=pltpu.PrefetchScalarGridSpec(
            num_scalar_prefetch=1, grid=(lay.nblk,), in_specs=in_specs, out_specs=out_specs),
        out_shape=out_shape,
        compiler_params=_cparams("parallel"),
        name="resid_norm",
    )(*args)


def _resid_kernel(meta_ref, x_ref, f_ref, g_ref, mod_ref, *rest, with_next):
    if with_next:
        gn_ref, modn_ref, xo_ref, hn_ref = rest
    else:
        (xo_ref,) = rest
    x = x_ref[...] + mod_ref[5:6, :] * _rms(f_ref[...], g_ref[3:4, :])
    xo_ref[...] = x
    if with_next:
        hn_ref[...] = _modulated(x, gn_ref[0:1, :], modn_ref, 0, 1).astype(hn_ref.dtype)


def _resid(lay, x, f, g4, mods, next_g4=None, next_mods=None):
    m, d = x.shape
    tb = lay.tb
    row = lambda i, mt: (i, 0)
    with_next = next_g4 is not None
    g_spec = pl.BlockSpec((4, d), lambda i, mt: (0, 0))
    mod_spec = pl.BlockSpec((None, 6, d), lambda i, mt: (mt[0, i], 0, 0))
    in_specs = [pl.BlockSpec((tb, d), row), pl.BlockSpec((tb, d), row), g_spec, mod_spec]
    args = [lay.meta, x, f, g4, mods]
    out_specs, out_shape = pl.BlockSpec((tb, d), row), jax.ShapeDtypeStruct((m, d), F32)
    if with_next:
        in_specs += [g_spec, mod_spec]
        args += [next_g4, next_mods]
        out_specs = [out_specs, pl.BlockSpec((tb, d), row)]
        out_shape = [out_shape, jax.ShapeDtypeStruct((m, d), BF16)]
    return pl.pallas_call(
        functools.partial(_resid_kernel, with_next=with_next),
        grid_spec=pltpu.PrefetchScalarGridSpec(
            num_scalar_prefetch=1, grid=(lay.nblk,), in_specs=in_specs, out_specs=out_specs),
        out_shape=out_shape,
        compiler_params=_cparams("parallel"),
        name="resid",
    )(*args)


def _mm_kernel(a_ref, w_ref, *rest, act, has_scale):
    if has_scale:
        s_ref, o_ref = rest
    else:
        (o_ref,) = rest
    a = a_ref[...]
    if act == "tanh":
        a = jnp.tanh(a)
    elif act == "sigmoid":
        a = jax.nn.sigmoid(a)
    out = _dot(a.astype(BF16), w_ref[...].astype(BF16))
    if has_scale:
        out = out * s_ref[...]
    o_ref[...] = out.astype(o_ref.dtype)


def _mm(a, w, lead=(), *, out_dtype=F32, act=None, a_col=0, n_groups=1, col_scale=None, tm=None, tn=None,
        a_single_buffer=False):
    m = a.shape[0]
    k, n = w.shape[-2:]
    tm = tm or _pick(m, (1024, 512, 256, 128, 64, 32, 16, 8))
    tn = tn or _pick(n, (512, 256, 128))
    nl = len(lead)
    if n_groups > 1:
        grid = (m // tm, n_groups, n // tn)
        a_spec = pl.BlockSpec((tm, k), lambda i, g, j: (i, g))
        w_spec = pl.BlockSpec((None,) * (nl + 1) + (k, tn), lambda i, g, j: lead + (g, 0, j))
        o_spec = pl.BlockSpec((tm, tn), lambda i, g, j: (i, g * (n // tn) + j))
        s_spec = pl.BlockSpec((1, tn), lambda i, g, j: (0, g * (n // tn) + j))
        sem = ("parallel", "arbitrary", "arbitrary")
    else:
        grid = (m // tm, n // tn)
        a_spec = pl.BlockSpec((tm, k), lambda i, j: (i, a_col),
                              **({"pipeline_mode": pl.Buffered(1)} if a_single_buffer else {}))
        w_spec = pl.BlockSpec((None,) * nl + (k, tn), lambda i, j: lead + (0, j))
        o_spec = pl.BlockSpec((tm, tn), lambda i, j: (i, j))
        s_spec = pl.BlockSpec((1, tn), lambda i, j: (0, j))
        sem = ("parallel", "arbitrary")
    in_specs, args = [a_spec, w_spec], [a, w]
    if col_scale is not None:
        in_specs.append(s_spec)
        args.append(col_scale)
    return pl.pallas_call(
        functools.partial(_mm_kernel, act=act, has_scale=col_scale is not None),
        grid=grid, in_specs=in_specs, out_specs=o_spec,
        out_shape=jax.ShapeDtypeStruct((m, n * n_groups), out_dtype),
        compiler_params=_cparams(*sem),
        name="mm",
    )(*args)


def _swiglu_kernel(a_ref, wg_ref, wu_ref, o_ref):
    a = a_ref[...]
    g = _dot(a, wg_ref[...].astype(BF16))
    u = _dot(a, wu_ref[...].astype(BF16))
    o_ref[...] = (_silu(g) * u).astype(o_ref.dtype)


def _swiglu_hidden(a, w_gu, lead):
    m, d = a.shape
    f = w_gu.shape[-1] // 2
    tm = _pick(m, (1024, 512, 256, 128, 64, 32, 16, 8))
    tn = _pick(f, (256, 128))
    nf = f // tn
    wlead = (None,) * len(lead)
    return pl.pallas_call(
        _swiglu_kernel,
        grid=(m // tm, nf),
        in_specs=[pl.BlockSpec((tm, d), lambda i, j: (i, 0)),
                  pl.BlockSpec(wlead + (d, tn), lambda i, j: lead + (0, j)),
                  pl.BlockSpec(wlead + (d, tn), lambda i, j: lead + (0, nf + j))],
        out_specs=pl.BlockSpec((tm, tn), lambda i, j: (i, j)),
        out_shape=jax.ShapeDtypeStruct((m, f), BF16),
        compiler_params=_cparams("parallel", "arbitrary"),
        name="swiglu",
    )(a, w_gu, w_gu)


MOE_ROW_TILE = 768
MOE_GATHER_ROWS = 256


def _moe_plan(route, n_experts, tm):
    m = route.shape[0]
    ids = route[:, :TOP_K].astype(jnp.int32).reshape(-1)
    onehot = (ids[:, None] == jnp.arange(n_experts, dtype=jnp.int32)[None, :]).astype(jnp.int32)
    csum = jnp.cumsum(onehot, axis=0)
    counts = csum[-1]
    rank = jnp.sum(onehot * csum, axis=1) - 1
    tiles_per = (counts + tm - 1) // tm
    tile_end = jnp.cumsum(tiles_per)
    row_start = (tile_end - tiles_per) * tm
    pos = jnp.sum(onehot * row_start[None, :], axis=1) + rank
    n_tiles = (TOP_K * m) // tm + n_experts
    row_token = jnp.zeros((n_tiles * tm,), jnp.int32).at[pos].set(jnp.arange(TOP_K * m, dtype=jnp.int32) // TOP_K)
    t = jnp.arange(n_tiles, dtype=jnp.int32)
    n_used = tile_end[-1]
    tile_expert = jnp.sum((t[:, None] >= tile_end[None, :]).astype(jnp.int32), axis=1)
    last_expert = jnp.sum((n_used - 1 >= tile_end).astype(jnp.int32))
    tile_expert = jnp.where(t < n_used, tile_expert, last_expert)
    return row_token, tile_expert, n_used.reshape(1), pos.reshape(m, TOP_K)


def _moe_gather_kernel(nu_ref, tok_ref, tok_next_ref, h_hbm, o_ref, buf, sem, *, tiles_per_row_tile):
    i = pl.program_id(0)
    tg = buf.shape[1]
    n_active = nu_ref[0] * tiles_per_row_tile
    slot = i % 2

    def copy(idx_ref, k, s):
        return pltpu.make_async_copy(h_hbm.at[pl.ds(idx_ref[0, k], 1), :], buf.at[s, pl.ds(k, 1), :], sem.at[s])

    def start_tile(idx_ref, s):
        def start(k2, carry):
            copy(idx_ref, 2 * k2, s).start(priority=0)
            copy(idx_ref, 2 * k2 + 1, s).start(priority=1)
            return carry

        lax.fori_loop(0, tg // 2, start, 0)

    @pl.when(i == 0)
    def _():
        start_tile(tok_ref, 0)

    @pl.when(i + 1 < n_active)
    def _():
        start_tile(tok_next_ref, 1 - slot)

    @pl.when(i < n_active)
    def _():
        def wait(k, carry):
            copy(tok_ref, k, slot).wait()
            return carry

        lax.fori_loop(0, tg, wait, 0)
        o_ref[...] = buf[slot].astype(o_ref.dtype)

    @pl.when(i >= n_active)
    def _():
        o_ref[...] = jnp.zeros_like(o_ref)


def _moe_gather(h, row_token, n_used, tm):
    m, d = h.shape
    rows = row_token.shape[0]
    tg = _pick(tm, (MOE_GATHER_ROWS, 128, 64, 32, 16, 8))
    nt = rows // tg
    tok = row_token.reshape(nt, 1, tg)
    return pl.pallas_call(
        functools.partial(_moe_gather_kernel, tiles_per_row_tile=tm // tg),
        grid_spec=pltpu.PrefetchScalarGridSpec(
            num_scalar_prefetch=1, grid=(nt,),
            in_specs=[pl.BlockSpec((None, 1, tg), lambda i, nu: (i, 0, 0), memory_space=pltpu.SMEM),
                      pl.BlockSpec((None, 1, tg), lambda i, nu: (jnp.minimum(i + 1, nt - 1), 0, 0),
                                   memory_space=pltpu.SMEM),
                      pl.BlockSpec(memory_space=pl.ANY)],
            out_specs=pl.BlockSpec((tg, d), lambda i, nu: (i, 0)),
            scratch_shapes=[pltpu.VMEM((2, tg, d), F32), pltpu.SemaphoreType.DMA((2,))]),
        out_shape=jax.ShapeDtypeStruct((rows, d), BF16),
        compiler_params=_cparams("arbitrary"),
        name="moe_gather",
    )(n_used, tok, tok, h)


def _moe_swiglu_kernel(te_ref, nu_ref, a_ref, wg_ref, wu_ref, o_ref):
    @pl.when(pl.program_id(0) < nu_ref[0])
    def _():
        a = a_ref[...]
        g = _dot(a, wg_ref[...].astype(BF16))
        u = _dot(a, wu_ref[...].astype(BF16))
        o_ref[...] = (_silu(g) * u).astype(o_ref.dtype)

    @pl.when(pl.program_id(0) >= nu_ref[0])
    def _():
        o_ref[...] = jnp.zeros_like(o_ref)


def _moe_swiglu(a, w_gu, fi, tile_expert, n_used, tm):
    rows, d = a.shape
    f = w_gu.shape[-1] // 2
    tn = _pick(f, (256, 128))
    nf = f // tn
    return pl.pallas_call(
        _moe_swiglu_kernel,
        grid_spec=pltpu.PrefetchScalarGridSpec(
            num_scalar_prefetch=2, grid=(rows // tm, nf),
            in_specs=[pl.BlockSpec((tm, d), lambda t, j, te, nu: (t, 0)),
                      pl.BlockSpec((None, None, d, tn), lambda t, j, te, nu: (fi, te[t], 0, j)),
                      pl.BlockSpec((None, None, d, tn), lambda t, j, te, nu: (fi, te[t], 0, nf + j))],
            out_specs=pl.BlockSpec((tm, tn), lambda t, j, te, nu: (t, j))),
        out_shape=jax.ShapeDtypeStruct((rows, f), BF16),
        compiler_params=_cparams("parallel", "arbitrary"),
        name="moe_swiglu",
    )(tile_expert, n_used, a, w_gu, w_gu)


def _moe_down_kernel(te_ref, nu_ref, a_ref, w_ref, o_ref):
    @pl.when(pl.program_id(0) < nu_ref[0])
    def _():
        o_ref[...] = _dot(a_ref[...], w_ref[...].astype(BF16))

    @pl.when(pl.program_id(0) >= nu_ref[0])
    def _():
        o_ref[...] = jnp.zeros_like(o_ref)


def _moe_down(a, w_down, fi, tile_expert, n_used, tm):
    rows, f = a.shape
    d = w_down.shape[-1]
    tn = _pick(d, (512, 256, 128))
    return pl.pallas_call(
        _moe_down_kernel,
        grid_spec=pltpu.PrefetchScalarGridSpec(
            num_scalar_prefetch=2, grid=(rows // tm, d // tn),
            in_specs=[pl.BlockSpec((tm, f), lambda t, j, te, nu: (t, 0)),
                      pl.BlockSpec((None, None, f, tn), lambda t, j, te, nu: (fi, te[t], 0, j))],
            out_specs=pl.BlockSpec((tm, tn), lambda t, j, te, nu: (t, j))),
        out_shape=jax.ShapeDtypeStruct((rows, d), F32),
        compiler_params=_cparams("parallel", "arbitrary"),
        name="moe_down",
    )(tile_expert, n_used, a, w_down)


def _resid_moe_kernel(meta_ref, pos_ref, pos_next_ref, x_ref, route_ref, g_ref, mod_ref, y_hbm, xo_ref, buf, sem):
    i = pl.program_id(0)
    tb = x_ref.shape[0]
    slot = i % 2

    def copy(idx_ref, k, e, s):
        return pltpu.make_async_copy(y_hbm.at[pl.ds(idx_ref[0, TOP_K * k + e], 1), :],
                                     buf.at[s, e, pl.ds(k, 1), :], sem.at[s])

    def start_tile(idx_ref, s):
        def start(k, carry):
            for e in range(TOP_K):
                copy(idx_ref, k, e, s).start(priority=e % 2)
            return carry

        lax.fori_loop(0, tb, start, 0)

    @pl.when(i == 0)
    def _():
        start_tile(pos_ref, 0)

    @pl.when(i + 1 < pl.num_programs(0))
    def _():
        start_tile(pos_next_ref, 1 - slot)

    def wait(k, carry):
        for e in range(TOP_K):
            copy(pos_ref, k, e, slot).wait()
        return carry

    lax.fori_loop(0, tb, wait, 0)
    route = route_ref[...]
    f = route[:, 2:3] * buf[slot, 0] + route[:, 3:4] * buf[slot, 1]
    xo_ref[...] = x_ref[...] + mod_ref[5:6, :] * _rms(f, g_ref[3:4, :])


def _resid_moe(lay, x, y_sorted, pos, route, g4, mods):
    m, d = x.shape
    tb = lay.tb
    row = lambda i, mt: (i, 0)
    pos3 = pos.reshape(lay.nblk, 1, TOP_K * tb)
    return pl.pallas_call(
        _resid_moe_kernel,
        grid_spec=pltpu.PrefetchScalarGridSpec(
            num_scalar_prefetch=1, grid=(lay.nblk,),
            in_specs=[pl.BlockSpec((None, 1, TOP_K * tb), lambda i, mt: (i, 0, 0), memory_space=pltpu.SMEM),
                      pl.BlockSpec((None, 1, TOP_K * tb), lambda i, mt: (jnp.minimum(i + 1, lay.nblk - 1), 0, 0),
                                   memory_space=pltpu.SMEM),
                      pl.BlockSpec((tb, d), row), pl.BlockSpec((tb, LANES), row),
                      pl.BlockSpec((4, d), lambda i, mt: (0, 0)),
                      pl.BlockSpec((None, 6, d), lambda i, mt: (mt[0, i], 0, 0)),
                      pl.BlockSpec(memory_space=pl.ANY)],
            out_specs=pl.BlockSpec((tb, d), row),
            scratch_shapes=[pltpu.VMEM((2, TOP_K, tb, d), F32), pltpu.SemaphoreType.DMA((2,))]),
        out_shape=jax.ShapeDtypeStruct((m, d), F32),
        compiler_params=_cparams("arbitrary"),
        name="resid_moe",
    )(lay.meta, pos3, pos3, x, route, g4, mods, y_sorted)


def _softmax_rows(s):
    e = jnp.exp(s - jnp.max(s, axis=-1, keepdims=True))
    return e * (1.0 / jnp.sum(e, axis=-1, keepdims=True))


def _attn_ctx_kernel(q_ref, k_ref, v_ref, _, o_ref, *, heads, scale):
    for h in range(heads):
        sl = slice(h * ATTN_HEAD_DIM, (h + 1) * ATTN_HEAD_DIM)
        s = _dot_nt(q_ref[:, sl].astype(BF16), k_ref[:, sl].astype(BF16)) * scale
        p = _softmax_rows(s).astype(BF16)
        o_ref[:, sl] = _dot(p, v_ref[:, sl].astype(BF16)).astype(o_ref.dtype)


def _attn_context(qkv, n_seq, seq_len, d, o_buf):
    n_heads = d // ATTN_HEAD_DIM
    hb = _pick(n_heads, (8, 4, 2, 1))
    wb = hb * ATTN_HEAD_DIM
    ncb = d // wb
    return pl.pallas_call(
        functools.partial(_attn_ctx_kernel, heads=hb, scale=ATTN_HEAD_DIM ** -0.5),
        grid=(n_seq, ncb),
        in_specs=[pl.BlockSpec((seq_len, wb), lambda b, h: (b, h)),
                  pl.BlockSpec((seq_len, wb), lambda b, h: (b, ncb + h)),
                  pl.BlockSpec((seq_len, wb), lambda b, h: (b, 2 * ncb + h)),
                  pl.BlockSpec(memory_space=pl.ANY)],
        out_specs=pl.BlockSpec((seq_len, wb), lambda b, h: (b, h)),
        out_shape=jax.ShapeDtypeStruct(o_buf.shape, o_buf.dtype),
        input_output_aliases={3: 0},
        compiler_params=_cparams("parallel", "parallel"),
        name="attn_context",
    )(qkv, qkv, qkv, o_buf)


def _attn_lat_kernel(q_ref, k_ref, v_ref, ck_ref, cv_ref, tb_ref, _, o_ref, *, rows, wr, scale):
    kb = k_ref[...].astype(BF16)
    vb = v_ref[...].astype(BF16)
    ckb = ck_ref[...].astype(BF16)
    cvb = cv_ref[...].astype(BF16)
    group = ATTN_LAT_ROW_GROUP if rows % ATTN_LAT_ROW_GROUP == 0 else 1
    row_start = [min(max(r - wr // 2, 0), rows - wr) for r in range(rows)]
    neg = jnp.full((GRID_W, GRID_W), NEG_INF, F32)
    biases = {}
    for r0 in range(0, rows, group):
        qrows = range(r0, r0 + group)
        k0 = row_start[r0]
        k1 = row_start[r0 + group - 1] + wr
        pattern = tuple(tuple(kr - r + MAX_WIN_ROWS - 1 if row_start[r] <= kr < row_start[r] + wr else None
                              for kr in range(k0, k1)) for r in qrows)
        if pattern not in biases:
            biases[pattern] = jnp.concatenate(
                [jnp.concatenate([neg if dr is None else tb_ref[dr] for dr in prow], axis=1) for prow in pattern],
                axis=0)
        q = q_ref[r0 * GRID_W:(r0 + group) * GRID_W, :].astype(BF16)
        kl = kb[k0 * GRID_W:k1 * GRID_W]
        vl = vb[k0 * GRID_W:k1 * GRID_W]
        s_loc = _dot_nt(q, kl) * scale + biases[pattern]
        s_ctx = _dot_nt(q, ckb) * scale
        mx = jnp.maximum(jnp.max(s_loc, axis=-1, keepdims=True), jnp.max(s_ctx, axis=-1, keepdims=True))
        e_loc = jnp.exp(s_loc - mx)
        e_ctx = jnp.exp(s_ctx - mx)
        inv = 1.0 / (jnp.sum(e_loc, axis=-1, keepdims=True) + jnp.sum(e_ctx, axis=-1, keepdims=True))
        o = (_dot(e_loc.astype(BF16), vl) + _dot(e_ctx.astype(BF16), cvb)) * inv
        o_ref[r0 * GRID_W:(r0 + group) * GRID_W, :] = o.astype(o_ref.dtype)


def _rel_bias_table(rpb):
    qc = np.arange(GRID_W)[:, None]
    kc = np.arange(GRID_W)[None, :]
    ws = np.clip(qc - WIN_COLS // 2, 0, GRID_W - WIN_COLS)
    in_win = (kc >= ws) & (kc < ws + WIN_COLS)
    dcol = np.clip(kc - qc, 1 - WIN_COLS, WIN_COLS - 1) + WIN_COLS - 1
    return jnp.where(in_win[None, None], rpb.astype(F32)[:, :, dcol], NEG_INF)


def _attn_latent(qkv, cache_k, cache_v, mi, rpb, row_block0, n_seq, seq_len, d, o_buf):
    n_heads = d // ATTN_HEAD_DIM
    rows = seq_len // GRID_W
    wr = min(MAX_WIN_ROWS, rows)
    past = cache_k.shape[2]
    ck = cache_k.reshape(cache_k.shape[0], cache_k.shape[1], past, d)
    cv = cache_v.reshape(cache_v.shape[0], cache_v.shape[1], past, d)
    table = _rel_bias_table(rpb)
    hd = ATTN_HEAD_DIM
    return pl.pallas_call(
        functools.partial(_attn_lat_kernel, rows=rows, wr=wr, scale=ATTN_HEAD_DIM ** -0.5),
        grid=(n_seq, n_heads),
        in_specs=[pl.BlockSpec((seq_len, hd), lambda b, h: (row_block0 + b, h)),
                  pl.BlockSpec((seq_len, hd), lambda b, h: (row_block0 + b, n_heads + h)),
                  pl.BlockSpec((seq_len, hd), lambda b, h: (row_block0 + b, 2 * n_heads + h)),
                  pl.BlockSpec((None, None, past, hd), lambda b, h: (b, mi, 0, h)),
                  pl.BlockSpec((None, None, past, hd), lambda b, h: (b, mi, 0, h)),
                  pl.BlockSpec((None, 2 * MAX_WIN_ROWS - 1, GRID_W, GRID_W), lambda b, h: (h, 0, 0, 0)),
                  pl.BlockSpec(memory_space=pl.ANY)],
        out_specs=pl.BlockSpec((seq_len, hd), lambda b, h: (row_block0 + b, h)),
        out_shape=jax.ShapeDtypeStruct(o_buf.shape, o_buf.dtype),
        input_output_aliases={6: 0},
        compiler_params=_cparams("parallel", "parallel"),
        name="attn_latent",
    )(qkv, qkv, qkv, ck, cv, table, o_buf)


def _halo_rows(meta_ref, h_prev_blk, h_next_blk):
    i = pl.program_id(0)
    hp = jnp.where(meta_ref[1, i] == 1, 0.0, h_prev_blk[POOL_HALO - 1:POOL_HALO, :])
    hn = jnp.where(meta_ref[2, i] == 1, 0.0, h_next_blk[0:1, :])
    return hp, hn


def _rwkv_prep_kernel(meta_ref, x_ref, xp_ref, xn_ref, g_ref, mod_ref, mu_ref, *o_refs):
    g = g_ref[0:1, :]
    h = _modulated(x_ref[...], g, mod_ref, 0, 1)
    hp, hn = _halo_rows(meta_ref, _modulated(xp_ref[...], g, mod_ref, 0, 1),
                        _modulated(xn_ref[...], g, mod_ref, 0, 1))
    tb = h.shape[0]
    row = lax.broadcasted_iota(jnp.int32, h.shape, 0)
    prev = jnp.where(row == 0, hp, pltpu.roll(h, 1, 0))
    nxt = jnp.where(row == tb - 1, hn, pltpu.roll(h, tb - 1, 0))
    xx = 0.5 * (prev + nxt) - h
    for n, o_ref in enumerate(o_refs):
        o_ref[...] = (h + xx * mu_ref[n:n + 1, :]).astype(o_ref.dtype)


def _halo_specs(lay, d):
    tb = lay.tb
    per = tb // POOL_HALO
    last = lay.nblk * per - 1
    return [pl.BlockSpec((tb, d), lambda i, mt: (i, 0)),
            pl.BlockSpec((POOL_HALO, d), lambda i, mt: (jnp.maximum(i * per - 1, 0), 0)),
            pl.BlockSpec((POOL_HALO, d), lambda i, mt: (jnp.minimum((i + 1) * per, last), 0))]


def _rwkv_prep(lay, x, g4, mods, mu):
    m, d = x.shape
    tb = lay.tb
    return pl.pallas_call(
        _rwkv_prep_kernel,
        grid_spec=pltpu.PrefetchScalarGridSpec(
            num_scalar_prefetch=1, grid=(lay.nblk,),
            in_specs=_halo_specs(lay, d) + [
                pl.BlockSpec((4, d), lambda i, mt: (0, 0)),
                pl.BlockSpec((None, 6, d), lambda i, mt: (mt[0, i], 0, 0)),
                pl.BlockSpec((6, d), lambda i, mt: (0, 0))],
            out_specs=[pl.BlockSpec((tb, d), lambda i, mt: (i, 0))] * 6),
        out_shape=[jax.ShapeDtypeStruct((m, d), BF16)] * 6,
        compiler_params=_cparams("parallel"),
        name="rwkv_prep",
    )(lay.meta, x, x, x, g4, mods, mu)


def _split(x):
    hi = x.astype(BF16)
    return hi, (x - hi.astype(F32)).astype(BF16)


def _dot3(a, b):
    (ah, al), (bh, bl) = a, b
    return _dot(jnp.concatenate([ah, ah, al], axis=1), jnp.concatenate([bh, bl, bh], axis=0))


def _dot3_nt(a, b):
    (ah, al), (bh, bl) = a, b
    return _dot_nt(jnp.concatenate([ah, ah, al], axis=1), jnp.concatenate([bh, bl, bh], axis=1))


def _rwkv_scan_kernel(*refs, seq_len, has_s0, pairs):
    if has_s0:
        (r_ref, k_ref, v_ref, w0_ref, w1_ref, a0_ref, a1_ref, g_ref, pv_ref, s0_ref, _, y_ref, yf_ref, yb_ref) = refs
        sf_ref = None
    else:
        (r_ref, k_ref, v_ref, w0_ref, w1_ref, a0_ref, a1_ref, g_ref, pv_ref, _, y_ref, sf_ref, yf_ref, yb_ref) = refs
        s0_ref = None
    c = RWKV_CHUNK
    n = RWKV_HEAD_DIM
    nchunks = seq_len // c
    wpre_refs = (w0_ref, w1_ref)
    apre_refs = (a0_ref, a1_ref)
    def iota(shape, dim):
        return lax.broadcasted_iota(jnp.int32, shape, dim)

    head0 = iota((c, LANES), 1) < n
    seg = (iota((LANES, LANES), 0) // n == iota((LANES, LANES), 1) // n)
    seg_f = seg.astype(F32)
    seg_b = seg_f.astype(BF16)
    tt = iota((c, 2 * c), 0)
    ss = iota((c, 2 * c), 1) % c
    lo_half = iota((c, 2 * c), 1) < c
    strict = (ss < tt, ss > tt)
    incl = (ss <= tt, ss >= tt)
    t2 = iota((c, c), 0)
    s2 = iota((c, c), 1)
    tri = ((s2 <= t2).astype(F32).astype(BF16), (s2 >= t2).astype(F32).astype(BF16))

    def seg_sum(x):
        hi, lo = _split(x)
        return _dot(jnp.concatenate([hi, lo], axis=1), jnp.concatenate([seg_b, seg_b], axis=0))

    head0_b = head0.astype(F32).astype(BF16)
    head1_b = (1.0 - head0.astype(F32)).astype(BF16)
    lo_half_b = lo_half.astype(F32).astype(BF16)
    hi_half_b = (1.0 - lo_half.astype(F32)).astype(BF16)

    def stack2(x):
        return jnp.concatenate([x * head0_b, x * head1_b], axis=0)

    def stack2_parts(parts):
        return tuple(stack2(p) for p in parts)

    def blockdiag(x):
        return jnp.concatenate([x * lo_half_b, x * hi_half_b], axis=0)

    def cat0(a, b):
        return tuple(jnp.concatenate([x, y], axis=0) for x, y in zip(a, b))

    def features(rows, lanes, d):
        k = k_ref[rows, lanes]
        kk = k * pv_ref[0:1, lanes]
        kk = kk / jnp.maximum(jnp.sqrt(seg_sum(kk * kk)), 1e-12)
        wx = -(pv_ref[5 + d:6 + d, lanes] + wpre_refs[d][rows, lanes])
        w_log = -(jnp.maximum(wx, 0.0) + jnp.log(1.0 + jnp.exp(-jnp.abs(wx)))) - 0.5
        logw = -jnp.exp(w_log)
        a = jax.nn.sigmoid(pv_ref[7 + d:8 + d, lanes] + apre_refs[d][rows, lanes])
        kd = k * (1.0 + (a - 1.0) * pv_ref[1:2, lanes])
        return kk, kd, kk * a, logw

    def chunk(rows, lanes, d, s_bd):
        r = r_ref[rows, lanes]
        v = v_ref[rows, lanes]
        kk, kd, bb, logw = features(rows, lanes, d)
        yield
        l1 = logw.astype(BF16)
        l2 = (logw - l1.astype(F32)).astype(BF16)
        l3 = (logw - l1.astype(F32) - l2.astype(F32)).astype(BF16)
        cum = _dot(jnp.concatenate([tri[d]] * 3, axis=1), jnp.concatenate([l1, l2, l3], axis=0))
        yield
        total = cum[c - 1:c, :] if d == 0 else cum[0:1, :]
        e_inv = jnp.exp(-cum)
        e_rest = jnp.exp(total - cum)
        at = _split(kk * jnp.exp(cum - logw))
        rt = (r * jnp.exp(cum)).astype(BF16)
        kb = cat0(stack2_parts(_split(kd * e_inv)), stack2_parts(_split(bb * e_inv)))
        s_parts = _split(s_bd)
        kbs = cat0(kb, s_parts)
        sa = _dot3_nt(at, kbs)
        sr = _dot_nt(rt, kbs[0])
        yield
        zero = jnp.zeros((c, 2 * c), F32)
        m_cat = jnp.where(strict[d], sa[:, :2 * c], zero)
        l_cat = jnp.where(strict[d], sa[:, 2 * c:4 * c], zero)
        rk_cat = jnp.where(incl[d], sr[:, :2 * c], zero)
        rb_cat = jnp.where(incl[d], sr[:, 2 * c:4 * c], zero)
        y0 = sr[:, 4 * c:]
        v_st = stack2_parts(_split(v))
        x = sa[:, 4 * c:] + _dot3(_split(m_cat), v_st)
        yield
        p = -l_cat
        n_stage = c.bit_length() - 1
        for stage in range(n_stage):
            pp = _split(p)
            xs = stack2_parts(_split(x))
            if stage + 1 < n_stage:
                px = _dot3(pp, tuple(jnp.concatenate([xi, blockdiag(pi)], axis=1) for xi, pi in zip(xs, pp)))
                x = x + px[:, :LANES]
                p = px[:, LANES:]
            else:
                x = x + _dot3(pp, xs)
            yield
        u = x
        u_parts = _split(u)
        y = y0 + _dot(
            jnp.concatenate([rk_cat, rb_cat], axis=1).astype(BF16),
            jnp.concatenate([v_st[0], -stack2(u_parts[0])], axis=0))
        vu_t = _split(jnp.concatenate([v, -u], axis=0).T)
        kb_rest = cat0(_split(kd * e_rest), _split(bb * e_rest))
        s_new = s_bd * jnp.exp(total) + seg_f * _dot3(vu_t, kb_rest)
        return y, s_new

    def run_lockstep(gens):
        results = [None] * len(gens)
        active = list(range(len(gens)))
        while active:
            for i in list(active):
                try:
                    next(gens[i])
                except StopIteration as stop:
                    results[i] = stop.value
                    active.remove(i)
        return results

    chains = [(pi, d) for pi in range(pairs) for d in range(2)]
    if has_s0:
        z = jnp.zeros((n, n), F32)
        s_init = tuple(
            jnp.concatenate([jnp.concatenate([s0_ref[d, 2 * pi], z], axis=1),
                             jnp.concatenate([z, s0_ref[d, 2 * pi + 1]], axis=1)], axis=0) for pi, d in chains)
    else:
        s_init = tuple(jnp.zeros((LANES, LANES), F32) for _ in chains)

    def body(ci, carry):
        rows_fb = (pl.ds(pl.multiple_of(ci * c, c), c), pl.ds(pl.multiple_of((nchunks - 1 - ci) * c, c), c))
        lanes = [slice(pi * LANES, (pi + 1) * LANES) for pi, _ in chains]
        results = run_lockstep([chunk(rows_fb[d], ln, d, s_bd)
                                for (_, d), ln, s_bd in zip(chains, lanes, carry)])
        for (_, d), ln, (y, _) in zip(chains, lanes, results):
            (yf_ref, yb_ref)[d][rows_fb[d], ln] = y
        return tuple(s_bd for _, s_bd in results)

    s_fin = lax.fori_loop(0, nchunks, body, s_init)
    if sf_ref is not None:
        for (pi, d), s_bd in zip(chains, s_fin):
            sf_ref[d, 2 * pi] = s_bd[:n, :n]
            sf_ref[d, 2 * pi + 1] = s_bd[n:, n:]

    fr = 2 * c if seq_len % (2 * c) == 0 else c

    def finish_rows(rows, lanes):
        y = yf_ref[rows, lanes] + yb_ref[rows, lanes]
        a_sum = sum(jax.nn.sigmoid(pv_ref[7 + d:8 + d, lanes] + apre_refs[d][rows, lanes]) for d in range(2))
        kd_sum = k_ref[rows, lanes] * (2.0 + (a_sum - 2.0) * pv_ref[1:2, lanes])
        sums = seg_sum(jnp.concatenate([y, r_ref[rows, lanes] * kd_sum * pv_ref[2:3, lanes]], axis=0))
        yield
        yc = y - sums[:fr] * (1.0 / n)
        var = seg_sum(yc * yc) * (1.0 / n)
        yield
        y = yc * lax.rsqrt(var + LNX_EPS) * pv_ref[3:4, lanes] + pv_ref[4:5, lanes] + sums[fr:] * v_ref[rows, lanes]
        y_ref[rows, lanes] = (y * g_ref[rows, lanes]).astype(y_ref.dtype)

    def finish(ci, carry):
        rows = pl.ds(pl.multiple_of(ci * fr, fr), fr)
        run_lockstep([finish_rows(rows, slice(pi * LANES, (pi + 1) * LANES)) for pi in range(pairs)])
        return carry

    lax.fori_loop(0, seq_len // fr, finish, 0)


def _rwkv_scan(r, k, v, wpre, apre, g, pvec, *, row_block0, n_seq, seq_len, s0=None, mi=0, y_buf=None):
    m, d = r.shape
    n = RWKV_HEAD_DIM
    def fits(p, bufs):
        return (d // LANES) % p == 0 and (bufs * 8 + 2 + 1) * seq_len * p * LANES * 4 <= RWKV_VMEM_BUDGET_BYTES

    pairs, bufs = next((p, b) for p in (4, 2, 1) for b in (2, 1) if fits(p, b))
    wb = pairs * LANES
    nblk = d // wb
    seq_map = lambda b, p: (row_block0 + b, p)
    blk = pl.BlockSpec((seq_len, wb), seq_map)
    blk_in = blk if bufs == 2 else pl.BlockSpec((seq_len, wb), seq_map, pipeline_mode=pl.Buffered(1))
    in_specs = [blk_in] * 8 + [pl.BlockSpec((16, wb), lambda b, p: (0, p))]
    args = [r, k, v, wpre[0], wpre[1], apre[0], apre[1], g, pvec]
    y_shape = jax.ShapeDtypeStruct((m, d), BF16)
    if s0 is not None:
        in_specs.append(pl.BlockSpec((None, None, 2, 2 * pairs, n, n), lambda b, p: (b, mi, 0, p, 0, 0)))
        args.append(s0)
    in_specs.append(pl.BlockSpec(memory_space=pl.ANY))
    args.append(y_buf)
    aliases = {len(args) - 1: 0}
    if s0 is not None:
        out_specs, out_shape = blk, y_shape
    else:
        out_specs = [blk, pl.BlockSpec((None, 2, 2 * pairs, n, n), lambda b, p: (b, 0, p, 0, 0))]
        out_shape = [y_shape, jax.ShapeDtypeStruct((n_seq, 2, d // n, n, n), F32)]
    return pl.pallas_call(
        functools.partial(_rwkv_scan_kernel, seq_len=seq_len, has_s0=s0 is not None, pairs=pairs),
        grid=(n_seq, nblk),
        in_specs=in_specs, out_specs=out_specs, out_shape=out_shape, input_output_aliases=aliases,
        scratch_shapes=[pltpu.VMEM((seq_len, wb), F32), pltpu.VMEM((seq_len, wb), F32)],
        compiler_params=_cparams("parallel", "parallel"),
        name="rwkv_scan",
    )(*args)


def _pool_prep_kernel(meta_ref, x_ref, xp_ref, xn_ref, g_ref, mod_ref, o_ref):
    i = pl.program_id(0)
    g = g_ref[0:1, :]
    h = _modulated(x_ref[...], g, mod_ref, 0, 1)
    hp = jnp.where(meta_ref[1, i] == 1, 0.0, _modulated(xp_ref[...], g, mod_ref, 0, 1))
    hn = jnp.where(meta_ref[2, i] == 1, 0.0, _modulated(xn_ref[...], g, mod_ref, 0, 1))
    tb, d = h.shape
    pd = d // len(POOL_WINDOWS)
    ext = jnp.concatenate([hp, h, hn], axis=0)
    ne = tb + 2 * POOL_HALO
    row = lax.broadcasted_iota(jnp.int32, (tb, pd), 0)
    at_first = (meta_ref[1, i] == 1).astype(jnp.int32)
    at_last = (meta_ref[2, i] == 1).astype(jnp.int32)
    acc = ext[:, 0:d] + pltpu.roll(ext, 1, 0)
    half = 1
    for gi, win in enumerate(POOL_WINDOWS):
        if gi > 0:
            sub = acc[:, pd:]
            acc = pltpu.roll(sub, half, 0) + pltpu.roll(sub, ne - half, 0)
            half *= 2
        wsum = acc[POOL_HALO:POOL_HALO + tb, 0:pd]
        missing = (at_first * jnp.maximum(win // 2 - row, 0) + at_last * jnp.maximum(row + win // 2 - tb, 0))
        cnt = (win - missing).astype(F32)
        o_ref[:, gi * pd:(gi + 1) * pd] = (wsum / cnt - h[:, gi * pd:(gi + 1) * pd]).astype(o_ref.dtype)


def _pool_prep(lay, x, g4, mods):
    m, d = x.shape
    tb = lay.tb
    return pl.pallas_call(
        _pool_prep_kernel,
        grid_spec=pltpu.PrefetchScalarGridSpec(
            num_scalar_prefetch=1, grid=(lay.nblk,),
            in_specs=_halo_specs(lay, d) + [
                pl.BlockSpec((4, d), lambda i, mt: (0, 0)),
                pl.BlockSpec((None, 6, d), lambda i, mt: (mt[0, i], 0, 0))],
            out_specs=pl.BlockSpec((tb, d), lambda i, mt: (i, 0))),
        out_shape=jax.ShapeDtypeStruct((m, d), BF16),
        compiler_params=_cparams("parallel"),
        name="pool_prep",
    )(lay.meta, x, x, x, g4, mods)


def kernel(x_prompt, x_sample, cache_k, cache_v, state_rwkv, c, c_ctx, ada_w, ada_b, norm_g,
           attn_w_qkv, attn_w_o, attn_rpb, rwkv_mu, rwkv_w_r, rwkv_w_k, rwkv_w_v, rwkv_w_o,
           rwkv_w0, rwkv_w1, rwkv_w2, rwkv_a0, rwkv_a1, rwkv_a2, rwkv_g1, rwkv_g2, rwkv_k_k, rwkv_k_a,
           rwkv_r_k, rwkv_lnx_w, rwkv_lnx_b, pool_w, pool_scale, ffn_w_gu, ffn_w_down,
           moe_router, moe_w_gu, moe_w_down):
    nb, seq, d = x_prompt.shape
    ndb, dseq, _ = x_sample.shape
    depth = ada_w.shape[0]
    n_experts = moe_router.shape[-1]
    lay = _Layout(nb, seq, ndb, dseq, _pick(np.gcd(seq, dseq), (TOK_BLOCK, 128, 64, 32, 16)))
    lay_prep = _Layout(nb, seq, ndb, dseq, _pick(np.gcd(seq, dseq), (TOK_BLOCK // 2, 64, 32, 16)))
    assert lay.mp % dseq == 0
    lat_block0 = lay.mp // dseq

    x = jnp.concatenate([x_prompt.reshape(lay.mp, d), x_sample.reshape(lay.ms, d)], axis=0)
    cond8 = jnp.zeros((8, d), F32).at[0].set(c_ctx).at[1:1 + ndb].set(c)
    mods_all = _ada_all(cond8, ada_w, ada_b).reshape(depth, 8, 6, d)

    new_k, new_v, new_s = [], [], []
    h_next = None
    for i in range(depth):
        kind, mi, fi = i % 3, i // 3, i // 2
        mods = mods_all[i]
        g4 = norm_g[i]
        if kind == 0:
            h = h_next if h_next is not None else _norm_mod(lay, x, g4, mods, g_row=0, shift_row=0, scale_row=1)
            qkv = _mm(h, attn_w_qkv, (mi,))
            o = _attn_context(qkv, nb, seq, d, jnp.zeros((lay.m, d), BF16))
            o = _attn_latent(qkv, cache_k, cache_v, mi, attn_rpb[mi], lat_block0, ndb, dseq, d, o)
            mix = _mm(o, attn_w_o, (mi,))
            heads = d // ATTN_HEAD_DIM
            new_k.append(qkv[:lay.mp, d:2 * d].reshape(nb, seq, heads, ATTN_HEAD_DIM))
            new_v.append(qkv[:lay.mp, 2 * d:].reshape(nb, seq, heads, ATTN_HEAD_DIM))
        elif kind == 1:
            xr, xw, xk, xv, xa, xg = _rwkv_prep(lay_prep, x, g4, mods, rwkv_mu[mi])
            r = _mm(xr, rwkv_w_r, (mi,))
            k = _mm(xk, rwkv_w_k, (mi,))
            v = _mm(xv, rwkv_w_v, (mi,))
            lw, la = rwkv_w1.shape[-1], rwkv_a1.shape[-1]
            w1cat = jnp.moveaxis(rwkv_w1[mi], 0, 1).reshape(d, 2 * lw)
            a1cat = jnp.moveaxis(rwkv_a1[mi], 0, 1).reshape(d, 2 * la)
            lg = rwkv_g1.shape[-1]
            lgp = -(-lg // LANES) * LANES
            g1p = jnp.pad(rwkv_g1[mi], ((0, 0), (0, lgp - lg)))
            g2p = jnp.pad(rwkv_g2[mi], ((0, lgp - lg), (0, 0)))
            tw = _mm(xw, w1cat)
            ta = _mm(xa, a1cat)
            tg = _mm(xg, g1p)
            wpre = [_mm(tw, rwkv_w2, (mi, dd), act="tanh", a_col=dd) for dd in range(2)]
            apre = [_mm(ta, rwkv_a2, (mi, dd), a_col=dd) for dd in range(2)]
            gate = _mm(tg, g2p, act="sigmoid")
            pvec = jnp.zeros((16, d), F32)
            for row, val in enumerate((rwkv_k_k[mi], rwkv_k_a[mi], rwkv_r_k[mi].reshape(d), rwkv_lnx_w[mi],
                                       rwkv_lnx_b[mi], rwkv_w0[mi, 0], rwkv_w0[mi, 1], rwkv_a0[mi, 0],
                                       rwkv_a0[mi, 1])):
                pvec = pvec.at[row].set(val)
            y, s_p = _rwkv_scan(r, k, v, wpre, apre, gate, pvec, row_block0=0, n_seq=nb, seq_len=seq,
                                y_buf=jnp.zeros((lay.m, d), BF16))
            y = _rwkv_scan(r, k, v, wpre, apre, gate, pvec, row_block0=lat_block0, n_seq=ndb, seq_len=dseq,
                           s0=state_rwkv, mi=mi, y_buf=y)
            mix = _mm(y, rwkv_w_o, (mi,))
            new_s.append(s_p)
        else:
            hd = _pool_prep(lay_prep, x, g4, mods)
            mix = _mm(hd, pool_w, (mi,), n_groups=len(POOL_WINDOWS), col_scale=pool_scale[mi].reshape(1, d))
        if i % 2 == 0:
            x, h = _resid_norm(lay, x, mix, g4, mods)
            hid = _swiglu_hidden(h, ffn_w_gu, (fi,))
            f = _mm(hid, ffn_w_down, (fi,), tn=_pick(d, (256, 128)), a_single_buffer=True)
        else:
            router = jnp.pad(moe_router[fi], ((0, 0), (0, LANES - n_experts)))
            x, h, route = _resid_norm(lay, x, mix, g4, mods, router=router, n_experts=n_experts)
            tm = _pick(TOP_K * lay.m, (MOE_ROW_TILE, 256, 128, 64, 32, 16, 8))
            row_token, tile_expert, n_used, pos = _moe_plan(route, n_experts, tm)
            xs = _moe_gather(h, row_token, n_used, tm)
            hid = _moe_swiglu(xs, moe_w_gu, fi, tile_expert, n_used, tm)
            ys = _moe_down(hid, moe_w_down, fi, tile_expert, n_used, tm)
            x = _resid_moe(lay, x, ys, pos, route, g4, mods)
            h_next = None
            continue
        if i + 1 < depth and (i + 1) % 3 == 0:
            x, h_next = _resid(lay, x, f, g4, mods, norm_g[i + 1], mods_all[i + 1])
        else:
            x, h_next = _resid(lay, x, f, g4, mods), None

    y_prompt = x[:lay.mp].reshape(nb, seq, d)
    y_sample = x[lay.mp:].reshape(ndb, dseq, d)
    return (y_prompt, y_sample, jnp.stack(new_k, axis=1), jnp.stack(new_v, axis=1), jnp.stack(new_s, axis=1))
```
